```python
import math
import jax, jax.numpy as jnp
from jax import lax
import numpy as np

D_MODEL = 1024
BATCH = 16
SEQ = 4096
DEPTH = 1

CONV_CH = 512
CONV_WIDTH = 31
N_HEADS = 8
HEAD_DIM = 64
ATTN_W = N_HEADS * HEAD_DIM
MOBA_BLOCK = 256
MOBA_TOPK = 3
Q_CHUNK = 128
N_EXPERTS = 32
TOP_K = 4
D_FF = D_MODEL
SWIGLU_LIMIT = 7.0
SWIGLU_ALPHA = 1.702
MOE_ROWS = 256
N_IN = 2 * CONV_CH + 3 * ATTN_W + 2 * D_MODEL
EPS = 1e-6

kernel_name = "hybrid_conformer_moba_moe_block"


def rms_norm(x, g):
    xf = x.astype(jnp.float32)
    y = xf * lax.rsqrt(jnp.mean(xf * xf, axis=-1, keepdims=True) + EPS)
    return (y * g.astype(jnp.float32)).astype(x.dtype)


def layer_norm(x, g, b):
    xf = x.astype(jnp.float32)
    mu = jnp.mean(xf, axis=-1, keepdims=True)
    var = jnp.mean(jnp.square(xf - mu), axis=-1, keepdims=True)
    y = (xf - mu) * lax.rsqrt(var + EPS)
    return (y * g.astype(jnp.float32) + b.astype(jnp.float32)).astype(x.dtype)


def alibi_slopes():
    return 2.0 ** (-8.0 * jnp.arange(1, N_HEADS + 1, dtype=jnp.float32) / N_HEADS)


def conformer_conv(u, dw, dw_b, ln_g, ln_b, w_o):
    a, g = jnp.split(u, 2, axis=-1)
    h = a * jax.nn.sigmoid(g)
    h = lax.conv_general_dilated(
        h, dw[:, None, :].astype(h.dtype), window_strides=(1,),
        padding=[(CONV_WIDTH - 1, 0)],
        dimension_numbers=("NWC", "WIO", "NWC"),
        feature_group_count=CONV_CH) + dw_b
    h = jax.nn.silu(layer_norm(h, ln_g, ln_b))
    return h @ w_o


def moba_attention(q, k, v, q_g, k_g):
    B, S = q.shape[0], q.shape[1]
    q = rms_norm(q, q_g).transpose(0, 2, 1, 3)
    k = rms_norm(k, k_g).transpose(0, 2, 1, 3)
    v = v.transpose(0, 2, 1, 3)
    nb = -(-S // MOBA_BLOCK)
    pad = nb * MOBA_BLOCK - S
    k = jnp.pad(k, ((0, 0), (0, 0), (0, pad), (0, 0)))
    v = jnp.pad(v, ((0, 0), (0, 0), (0, pad), (0, 0)))
    kb = k.reshape(B, N_HEADS, nb, MOBA_BLOCK, HEAD_DIM)
    vb = v.reshape(B, N_HEADS, nb, MOBA_BLOCK, HEAD_DIM)
    kmean = jnp.mean(kb, axis=3)
    topk = min(MOBA_TOPK, nb)
    slopes = alibi_slopes()
    scale = HEAD_DIM ** -0.5
    bi = jnp.arange(B)[:, None, None, None]
    hi = jnp.arange(N_HEADS)[None, :, None, None]
    blk_ids = jnp.arange(nb)
    kpos = jnp.arange(MOBA_BLOCK)

    def chunk(c):
        t0 = c * Q_CHUNK
        qc = lax.dynamic_slice_in_dim(q, t0, Q_CHUNK, axis=2)
        t = t0 + jnp.arange(Q_CHUNK)
        own = t0 // MOBA_BLOCK
        gate = jnp.einsum("bhqd,bhnd->bhqn", qc, kmean).astype(jnp.float32)
        gate = jnp.where(blk_ids < own, gate, -jnp.inf)
        _, sel = lax.top_k(gate, topk)
        sel_ok = sel < own
        ks = kb[bi, hi, sel]
        vs = vb[bi, hi, sel]
        s_past = jnp.einsum("bhqd,bhqrkd->bhqrk", qc, ks).astype(jnp.float32) * scale
        dist_past = (t[None, None, :, None, None] - (sel[..., None] * MOBA_BLOCK + kpos)).astype(jnp.float32)
        s_past = s_past - slopes[None, :, None, None, None] * dist_past
        s_past = jnp.where(sel_ok[..., None], s_past, -jnp.inf)
        ko = lax.dynamic_index_in_dim(kb, own, axis=2, keepdims=False)
        vo = lax.dynamic_index_in_dim(vb, own, axis=2, keepdims=False)
        s_own = jnp.einsum("bhqd,bhkd->bhqk", qc, ko).astype(jnp.float32) * scale
        dist_own = t[:, None] - (own * MOBA_BLOCK + kpos)[None, :]
        s_own = jnp.where(dist_own[None, None] >= 0,
                          s_own - slopes[None, :, None, None] * dist_own.astype(jnp.float32)[None, None],
                          -jnp.inf)
        s = jnp.concatenate([s_past.reshape(B, N_HEADS, Q_CHUNK, topk * MOBA_BLOCK), s_own], axis=-1)
        p = jax.nn.softmax(s, axis=-1).astype(v.dtype)
        p_past = p[..., :topk * MOBA_BLOCK].reshape(B, N_HEADS, Q_CHUNK, topk, MOBA_BLOCK)
        p_own = p[..., topk * MOBA_BLOCK:]
        return (jnp.einsum("bhqrk,bhqrkd->bhqd", p_past, vs)
                + jnp.einsum("bhqk,bhkd->bhqd", p_own, vo))

    out = lax.map(chunk, jnp.arange(S // Q_CHUNK))
    return out.transpose(1, 0, 3, 2, 4).reshape(B, S, ATTN_W)


def moe_ffn(xf, w_router, b_router, w_ug, b_ug, w_down, b_down):
    T = xf.shape[0]
    logits = (xf @ w_router + b_router).astype(jnp.float32)
    top_v, top_i = lax.top_k(logits, TOP_K)
    gates = jax.nn.softmax(top_v, axis=-1)
    n_assign = T * TOP_K
    flat_e = top_i.reshape(-1)
    flat_t = jnp.arange(n_assign, dtype=jnp.int32) // TOP_K
    flat_g = gates.reshape(-1)
    order = jnp.argsort(flat_e)
    se = flat_e[order]
    counts = jnp.bincount(flat_e, length=N_EXPERTS)
    starts = jnp.cumsum(counts) - counts
    padded = (counts + MOE_ROWS - 1) // MOE_ROWS * MOE_ROWS
    pad_end = jnp.cumsum(padded)
    pad_start = pad_end - padded
    dest = pad_start[se] + jnp.arange(n_assign) - starts[se]
    n_blocks = -(-n_assign // MOE_ROWS) + N_EXPERTS
    n_rows = n_blocks * MOE_ROWS
    row_tok = jnp.zeros((n_rows,), jnp.int32).at[dest].set(flat_t[order])
    row_gate = jnp.zeros((n_rows,), jnp.float32).at[dest].set(flat_g[order])
    blk_e = jnp.minimum(jnp.searchsorted(pad_end, jnp.arange(n_blocks) * MOE_ROWS, side="right"),
                        N_EXPERTS - 1)

    def run(args):
        tok, g, e = args
        h = xf[tok] @ w_ug[e] + b_ug[e]
        a = jnp.minimum(h[:, :D_FF], SWIGLU_LIMIT)
        lin = jnp.clip(h[:, D_FF:], -SWIGLU_LIMIT, SWIGLU_LIMIT)
        act = a * jax.nn.sigmoid(SWIGLU_ALPHA * a) * (lin + 1.0)
        y = act @ w_down[e] + b_down[e]
        return y * g[:, None].astype(y.dtype)

    ys = lax.map(run, (row_tok.reshape(n_blocks, MOE_ROWS),
                       row_gate.reshape(n_blocks, MOE_ROWS), blk_e))
    return jnp.zeros_like(xf).at[row_tok].add(ys.reshape(n_rows, -1))


def setup_inputs(seed: int = 0) -> dict:
    key = jax.random.key(seed)
    ks = jax.random.split(key, 20)
    L, D, C, A, E, F = DEPTH, D_MODEL, CONV_CH, ATTN_W, N_EXPERTS, D_FF
    n = jax.random.normal
    f32 = jnp.float32
    return {
        "x": n(ks[0], (BATCH, SEQ, D), f32),
        "norm1_g": 1.0 + 0.01 * n(ks[1], (L, D), f32),
        "w_in": n(ks[2], (L, D, N_IN), f32) * D ** -0.5,
        "b_gates": 0.01 * n(ks[3], (L, 2 * D), f32),
        "conv_dw": n(ks[4], (L, CONV_WIDTH, C), f32) * CONV_WIDTH ** -0.5,
        "conv_dw_b": 0.01 * n(ks[5], (L, C), f32),
        "conv_ln_g": 1.0 + 0.01 * n(ks[6], (L, C), f32),
        "conv_ln_b": 0.01 * n(ks[7], (L, C), f32),
        "w_conv_out": n(ks[8], (L, C, D), f32) * C ** -0.5,
        "q_norm_g": 1.0 + 0.01 * n(ks[9], (L, HEAD_DIM), f32),
        "k_norm_g": 1.0 + 0.01 * n(ks[10], (L, HEAD_DIM), f32),
        "w_attn_out": n(ks[11], (L, A, D), f32) * A ** -0.5,
        "w_out": n(ks[12], (L, D, D), f32) * D ** -0.5,
        "norm2_g": 1.0 + 0.01 * n(ks[13], (L, D), f32),
        "w_router": n(ks[14], (L, D, E), f32) * D ** -0.5,
        "b_router": 0.01 * n(ks[15], (L, E), f32),
        "w_up_gate": n(ks[16], (L, E, D, 2 * F), f32) * D ** -0.5,
        "b_up_gate": 0.01 * n(ks[17], (L, E, 2 * F), f32),
        "w_down": n(ks[18], (L, E, F, D), f32) * F ** -0.5,
        "b_down": 0.01 * n(ks[19], (L, E, D), f32),
    }


def reference(x, norm1_g, w_in, b_gates, conv_dw, conv_dw_b, conv_ln_g, conv_ln_b, w_conv_out,
              q_norm_g, k_norm_g, w_attn_out, w_out, norm2_g, w_router, b_router,
              w_up_gate, b_up_gate, w_down, b_down):
    B, S, D = x.shape
    h = x
    c1 = 2 * CONV_CH
    c2 = c1 + ATTN_W
    c3 = c2 + ATTN_W
    c4 = c3 + ATTN_W
    for l in range(DEPTH):
        u = rms_norm(h, norm1_g[l])
        proj = u @ w_in[l]
        y_conv = conformer_conv(proj[..., :c1], conv_dw[l], conv_dw_b[l],
                                conv_ln_g[l], conv_ln_b[l], w_conv_out[l])
        q = proj[..., c1:c2].reshape(B, S, N_HEADS, HEAD_DIM)
        k = proj[..., c2:c3].reshape(B, S, N_HEADS, HEAD_DIM)
        v = proj[..., c3:c4].reshape(B, S, N_HEADS, HEAD_DIM)
        y_attn = moba_attention(q, k, v, q_norm_g[l], k_norm_g[l]) @ w_attn_out[l]
        g = jax.nn.sigmoid(proj[..., c4:] + b_gates[l])
        mixed = g[..., :D] * y_conv + g[..., D:] * y_attn
        h = h + mixed @ w_out[l]
        u2 = rms_norm(h, norm2_g[l]).reshape(B * S, D)
        h = h + moe_ffn(u2, w_router[l], b_router[l], w_up_gate[l], b_up_gate[l],
                        w_down[l], b_down[l]).reshape(B, S, D)
    return h
```

```python
import functools

import jax
import jax.numpy as jnp
from jax import lax
from jax.experimental import pallas as pl
from jax.experimental.pallas import tpu as pltpu

F32 = jnp.float32
BF16 = jnp.bfloat16
I32 = jnp.int32
U32 = jnp.uint32

EPS = 1e-6
CONV_WIDTH = 31
HALO = 32
N_HEADS = 8
HEAD_DIM = 64
MOBA_BLOCK = 256
MOBA_TOPK = 3
GATE_SLOTS = 16
HEADS_PER_GROUP = 4
N_EXPERTS = 32
TOP_K = 4
SWIGLU_LIMIT = 7.0
SWIGLU_ALPHA = 1.702
EXPERT_ROWS = 512
NEG_INF = float("-inf")
VMEM_LIMIT = 56 * 1024 * 1024


def _cparams(sem):
    return pltpu.CompilerParams(dimension_semantics=sem, vmem_limit_bytes=VMEM_LIMIT)


def _dot(a, b):
    return jnp.dot(a, b, preferred_element_type=F32)


def _dot_nt(a, b):
    return lax.dot_general(a, b, (((1,), (1,)), ((), ())), preferred_element_type=F32)


def _split_bf16(x):
    hi = x.astype(BF16)
    lo = (x - hi.astype(F32)).astype(BF16)
    return hi, lo


def _sigmoid(x):
    return 1.0 / (1.0 + jnp.exp(-x))


def _inproj_kernel(tiles_per_seq, x_ref, g1_ref, wglu_ref, wq_ref, wk_ref, wv_ref, wg_ref, bg_ref,
                   qg_ref, kg_ref, bd_ref,
                   h_ref, q_ref, k_ref, v_ref, gate_ref, gt_ref, km_ref):
    i = pl.program_id(0)
    tm = x_ref.shape[0]
    c = h_ref.shape[1]
    aw = q_ref.shape[1]

    @pl.when(i == 0)
    def _():
        km_ref[...] = jnp.zeros_like(km_ref)

    x = x_ref[...]
    ms = jnp.mean(x * x, axis=-1, keepdims=True)
    u = (x * lax.rsqrt(ms + EPS) * g1_ref[...]).astype(BF16)

    glu = _dot(u, wglu_ref[...])
    h_ref[...] = (glu[:, :c] * _sigmoid(glu[:, c:])).astype(BF16)

    bd = bd_ref[...]

    def head_norm(t, g):
        hi, lo = _split_bf16(t * t)
        ss = _dot(hi, bd) + _dot(lo, bd)
        return t * lax.rsqrt(ss * (1.0 / HEAD_DIM) + EPS) * g

    qn = head_norm(_dot(u, wq_ref[...]), qg_ref[...]) * (HEAD_DIM ** -0.5)
    kn = head_norm(_dot(u, wk_ref[...]), kg_ref[...])
    q_hi, q_lo = _split_bf16(qn)
    q_ref[...] = q_hi
    k_ref[...] = kn.astype(BF16)
    v_ref[...] = _dot(u, wv_ref[...]).astype(BF16)
    gate_ref[...] = _sigmoid(_dot(u, wg_ref[...]) + bg_ref[...]).astype(BF16)

    lane_head = lax.broadcasted_iota(I32, (1, aw), 1) // HEAD_DIM
    blocks_per_tile = tm // MOBA_BLOCK
    n0 = (i % tiles_per_seq) * blocks_per_tile
    for j in range(blocks_per_tile):
        kmj = jnp.mean(kn[j * MOBA_BLOCK:(j + 1) * MOBA_BLOCK], axis=0, keepdims=True)
        for h in range(N_HEADS):
            km_ref[pl.ds(h * GATE_SLOTS + n0 + j, 1), :] = jnp.where(lane_head == h, kmj, 0.0)

    km_hi, km_lo = _split_bf16(km_ref[...])
    gt_ref[0] = _dot_nt(km_hi, q_hi) + _dot_nt(km_lo, q_hi) + _dot_nt(km_hi, q_lo)


def _inproj(x2, norm1_g, w_in, b_gates, q_norm_g, k_norm_g, seq, tm):
    t, d = x2.shape
    aw = N_HEADS * HEAD_DIM
    c = (w_in.shape[1] - 3 * aw - 2 * d) // 2
    c1, c2, c3, c4 = 2 * c, 2 * c + aw, 2 * c + 2 * aw, 2 * c + 3 * aw
    wb = w_in.astype(BF16)
    hd = jnp.arange(aw, dtype=I32) // HEAD_DIM
    bd = (hd[:, None] == hd[None, :]).astype(BF16)
    n_tiles = t // tm
    full = lambda a: pl.BlockSpec(a.shape, lambda i: (0,) * a.ndim)
    row = lambda n: pl.BlockSpec((tm, n), lambda i: (i, 0))
    args = (x2, norm1_g.reshape(1, d), wb[:, :c1], wb[:, c1:c2], wb[:, c2:c3], wb[:, c3:c4], wb[:, c4:],
            b_gates.reshape(1, 2 * d), jnp.tile(q_norm_g, N_HEADS).reshape(1, aw),
            jnp.tile(k_norm_g, N_HEADS).reshape(1, aw), bd)
    return pl.pallas_call(
        functools.partial(_inproj_kernel, seq // tm),
        grid=(n_tiles,),
        in_specs=[row(d)] + [full(a) for a in args[1:]],
        out_specs=[row(c), row(aw), row(aw), row(aw), row(2 * d),
                   pl.BlockSpec((1, N_HEADS * GATE_SLOTS, tm), lambda i: (i, 0, 0))],
        out_shape=[jax.ShapeDtypeStruct((t, c), BF16), jax.ShapeDtypeStruct((t, aw), BF16),
                   jax.ShapeDtypeStruct((t, aw), BF16), jax.ShapeDtypeStruct((t, aw), BF16),
                   jax.ShapeDtypeStruct((t, 2 * d), BF16),
                   jax.ShapeDtypeStruct((n_tiles, N_HEADS * GATE_SLOTS, tm), F32)],
        scratch_shapes=[pltpu.VMEM((N_HEADS * GATE_SLOTS, aw), F32)],
        compiler_params=_cparams(("arbitrary",)),
        name="inproj",
    )(*args)


def _conv_kernel(h_ref, dw_ref, dwb_ref, lng_ref, lnb_ref, o_ref, win_ref):
    s = pl.program_id(1)
    ts = h_ref.shape[1]

    @pl.when(s == 0)
    def _():
        win_ref[0:HALO, :] = jnp.zeros((HALO, win_ref.shape[1]), F32)

    @pl.when(s != 0)
    def _():
        win_ref[0:HALO, :] = win_ref[ts:ts + HALO, :]

    win_ref[HALO:HALO + ts, :] = h_ref[0].astype(F32)
    acc = jnp.zeros((ts, win_ref.shape[1]), F32) + dwb_ref[...]
    for w in range(CONV_WIDTH):
        off = HALO - (CONV_WIDTH - 1) + w
        acc = acc + dw_ref[w:w + 1, :] * win_ref[off:off + ts, :]
    mu = jnp.mean(acc, axis=-1, keepdims=True)
    cen = acc - mu
    var = jnp.mean(cen * cen, axis=-1, keepdims=True)
    y = cen * lax.rsqrt(var + EPS) * lng_ref[...] + lnb_ref[...]
    o_ref[0] = (y * _sigmoid(y)).astype(BF16)


def _conv(h3, dw, dwb, lng, lnb, ts):
    b, s, c = h3.shape
    vec = lambda a: pl.BlockSpec(a.shape, lambda i, j: (0, 0))
    args = (h3, dw, dwb.reshape(1, c), lng.reshape(1, c), lnb.reshape(1, c))
    return pl.pallas_call(
        _conv_kernel,
        grid=(b, s // ts),
        in_specs=[pl.BlockSpec((1, ts, c), lambda i, j: (i, j, 0))] + [vec(a) for a in args[1:]],
        out_specs=pl.BlockSpec((1, ts, c), lambda i, j: (i, j, 0)),
        out_shape=jax.ShapeDtypeStruct((b, s, c), BF16),
        scratch_shapes=[pltpu.VMEM((ts + HALO, c), F32)],
        compiler_params=_cparams(("arbitrary", "arbitrary")),
        name="conv",
    )(*args)


def _attn_kernel(q_ref, k_ref, v_ref, gt_ref, slope_ref, o_ref,
                 vt_ref, qbd_ref, alibi_ref, own_ref, mrow_ref, m_ref, l_ref, acc_ref):
    grp = pl.program_id(1)
    qb = pl.program_id(2)
    blk = MOBA_BLOCK
    hp = HEADS_PER_GROUP
    gw = hp * HEAD_DIM
    n_blocks = k_ref.shape[1] // blk
    row_head = lax.broadcasted_iota(I32, (gw, blk), 0) // HEAD_DIM

    @pl.when(qb == 0)
    def _():
        for n in range(n_blocks):
            vt = v_ref[0, n * blk:(n + 1) * blk, :].astype(F32).T.astype(BF16)
            for h in range(hp):
                vt_ref[n, :, h * blk:(h + 1) * blk] = jnp.where(row_head == h, vt, jnp.zeros_like(vt))
        kpos = lax.broadcasted_iota(I32, (blk, blk), 0)
        qpos = lax.broadcasted_iota(I32, (blk, blk), 1)
        for h in range(hp):
            a = kpos.astype(F32) * slope_ref[grp * hp + h]
            alibi_ref[:, h * blk:(h + 1) * blk] = a
            own_ref[:, h * blk:(h + 1) * blk] = jnp.where(kpos <= qpos, a, NEG_INF)

    qt = q_ref[0].astype(F32).T.astype(BF16)
    for h in range(hp):
        qbd_ref[:, h * blk:(h + 1) * blk] = jnp.where(row_head == h, qt, jnp.zeros_like(qt))

    slot = lax.broadcasted_iota(I32, (GATE_SLOTS, blk), 0)
    for h in range(hp):
        g = gt_ref[0, h * GATE_SLOTS:(h + 1) * GATE_SLOTS, :]
        g = jnp.where(slot < qb, g, NEG_INF)
        sel = jnp.zeros((GATE_SLOTS, blk), jnp.bool_)
        for _ in range(MOBA_TOPK):
            mx = jnp.max(g, axis=0, keepdims=True)
            first = jnp.min(jnp.where(g == mx, slot, GATE_SLOTS), axis=0, keepdims=True)
            pick = (slot == first) & (mx > NEG_INF)
            sel = sel | pick
            g = jnp.where(pick, NEG_INF, g)
        dist = ((qb - slot) * blk).astype(F32) * slope_ref[grp * hp + h]
        mrow_ref[:, h * blk:(h + 1) * blk] = jnp.where(sel, -dist, NEG_INF)

    def scores(n):
        kb = k_ref[0, pl.ds(pl.multiple_of(n * blk, blk), blk), :]
        return _dot(kb, qbd_ref[...])

    def pv(n, p):
        pcat = jnp.concatenate([p[:, h * blk:(h + 1) * blk] for h in range(hp)], axis=0).astype(BF16)
        return _dot(vt_ref[n], pcat)

    def expand(r):
        return jnp.concatenate(
            [jnp.broadcast_to(r[:, h * blk:(h + 1) * blk], (HEAD_DIM, blk)) for h in range(hp)], axis=0)

    t = scores(qb) + own_ref[...]
    m0 = jnp.max(t, axis=0, keepdims=True)
    p = jnp.exp(t - m0)
    m_ref[...] = m0
    l_ref[...] = jnp.sum(p, axis=0, keepdims=True)
    acc_ref[...] = pv(qb, p)

    def body(n, carry):
        t = scores(n) + alibi_ref[...] + mrow_ref[pl.ds(n, 1), :]
        m_old = m_ref[...]
        m_new = jnp.maximum(m_old, jnp.max(t, axis=0, keepdims=True))
        alpha = jnp.exp(m_old - m_new)
        p = jnp.exp(t - m_new)
        m_ref[...] = m_new
        l_ref[...] = alpha * l_ref[...] + jnp.sum(p, axis=0, keepdims=True)
        acc_ref[...] = expand(alpha) * acc_ref[...] + pv(n, p)
        return carry

    lax.fori_loop(0, qb, body, 0)
    out = acc_ref[...] / expand(l_ref[...])
    o_ref[0] = out.T.astype(BF16)


def _attention(q3, k3, v3, gt, slopes, tm):
    b, s, aw = q3.shape
    blk = MOBA_BLOCK
    gw = HEADS_PER_GROUP * HEAD_DIM
    n_groups = aw // gw
    nb = s // blk
    per_tile = tm // blk
    tiles_per_seq = s // tm
    return pl.pallas_call(
        _attn_kernel,
        grid_spec=pltpu.PrefetchScalarGridSpec(
            num_scalar_prefetch=0,
            grid=(b, n_groups, nb),
            in_specs=[
                pl.BlockSpec((1, blk, gw), lambda i, g, j: (i, j, g)),
                pl.BlockSpec((1, s, gw), lambda i, g, j: (i, 0, g)),
                pl.BlockSpec((1, s, gw), lambda i, g, j: (i, 0, g)),
                pl.BlockSpec((1, HEADS_PER_GROUP * GATE_SLOTS, blk),
                             lambda i, g, j: (i * tiles_per_seq + j // per_tile, g, j % per_tile)),
                pl.BlockSpec(memory_space=pltpu.SMEM),
            ],
            out_specs=pl.BlockSpec((1, blk, gw), lambda i, g, j: (i, j, g)),
            scratch_shapes=[
                pltpu.VMEM((nb, gw, HEADS_PER_GROUP * blk), BF16),
                pltpu.VMEM((gw, HEADS_PER_GROUP * blk), BF16),
                pltpu.VMEM((blk, HEADS_PER_GROUP * blk), F32),
                pltpu.VMEM((blk, HEADS_PER_GROUP * blk), F32),
                pltpu.VMEM((GATE_SLOTS, HEADS_PER_GROUP * blk), F32),
                pltpu.VMEM((1, HEADS_PER_GROUP * blk), F32),
                pltpu.VMEM((1, HEADS_PER_GROUP * blk), F32),
                pltpu.VMEM((gw, blk), F32),
            ],
        ),
        out_shape=jax.ShapeDtypeStruct((b, s, aw), BF16),
        compiler_params=_cparams(("arbitrary", "arbitrary", "arbitrary")),
        name="attn",
    )(q3, k3, v3, gt, slopes)


def _pack_bf16_pairs(x):
    n = x.shape[1] // 2
    r = x.astype(BF16).astype(F32)
    lo = lax.shift_right_logical(pltpu.bitcast(r[:, :n], U32), jnp.uint32(16))
    hi = pltpu.bitcast(r[:, n:], U32) & jnp.uint32(0xFFFF0000)
    return hi | lo


def _unpack_bf16_pairs(w):
    lo = pltpu.bitcast(lax.shift_left(w, jnp.uint32(16)), F32)
    hi = pltpu.bitcast(w & jnp.uint32(0xFFFF0000), F32)
    return jnp.concatenate([lo, hi], axis=1)


def _mixout_kernel(x_ref, c_ref, a_ref, gate_ref, wco_ref, wao_ref, wout_ref, g2_ref,
                   wrh_ref, wrl_ref, br_ref,
                   h1_ref, u2_ref, topi_ref, topg_ref):
    d = x_ref.shape[1]
    y_c = _dot(c_ref[...], wco_ref[...])
    y_a = _dot(a_ref[...], wao_ref[...])
    gate = gate_ref[...].astype(F32)
    mixed = (gate[:, :d] * y_c + gate[:, d:] * y_a).astype(BF16)
    h1 = x_ref[...] + _dot(mixed, wout_ref[...])
    h1_ref[...] = h1
    ms = jnp.mean(h1 * h1, axis=-1, keepdims=True)
    u2 = h1 * lax.rsqrt(ms + EPS) * g2_ref[...]
    u2_ref[...] = _pack_bf16_pairs(u2)
    u_hi, u_lo = _split_bf16(u2)
    wrh = wrh_ref[...]
    logits = _dot_nt(wrh, u_hi) + _dot_nt(wrl_ref[...], u_hi) + _dot_nt(wrh, u_lo) + br_ref[...]
    eid = lax.broadcasted_iota(I32, logits.shape, 0)
    vals = []
    for k in range(TOP_K):
        mx = jnp.max(logits, axis=0, keepdims=True)
        first = jnp.min(jnp.where(logits == mx, eid, N_EXPERTS), axis=0, keepdims=True)
        topi_ref[k:k + 1, :] = first
        vals.append(mx)
        logits = jnp.where(eid == first, NEG_INF, logits)
    es = [jnp.exp(v - vals[0]) for v in vals]
    den = es[0] + es[1] + es[2] + es[3]
    for k in range(TOP_K):
        topg_ref[k:k + 1, :] = es[k] / den


def _mixout(x2, c2, a2, gates, w_conv_out, w_attn_out, w_out, norm2_g, w_router, b_router, tm):
    t, d = x2.shape
    wr_t = w_router.T
    wrh = wr_t.astype(BF16)
    wrl = (wr_t - wrh.astype(F32)).astype(BF16)
    args = (x2, c2, a2, gates, w_conv_out.astype(BF16), w_attn_out.astype(BF16), w_out.astype(BF16),
            norm2_g.reshape(1, d), wrh, wrl, b_router.reshape(N_EXPERTS, 1))
    full = lambda a: pl.BlockSpec(a.shape, lambda i: (0,) * a.ndim)
    row = lambda a: pl.BlockSpec((tm, a.shape[1]), lambda i: (i, 0))
    return pl.pallas_call(
        _mixout_kernel,
        grid=(t // tm,),
        in_specs=[row(a) for a in args[:4]] + [full(a) for a in args[4:]],
        out_specs=[pl.BlockSpec((tm, d), lambda i: (i, 0)), pl.BlockSpec((tm, d // 2), lambda i: (i, 0)),
                   pl.BlockSpec((TOP_K, tm), lambda i: (0, i)), pl.BlockSpec((TOP_K, tm), lambda i: (0, i))],
        out_shape=[jax.ShapeDtypeStruct((t, d), F32), jax.ShapeDtypeStruct((t, d // 2), U32),
                   jax.ShapeDtypeStruct((TOP_K, t), I32), jax.ShapeDtypeStruct((TOP_K, t), F32)],
        compiler_params=_cparams(("arbitrary",)),
        name="mixout",
    )(*args)


def _rank_kernel(topi_ref, tri_ref, rank_ref, cnt_ref, run_ref):
    i = pl.program_id(0)
    tr = topi_ref.shape[1]

    @pl.when(i == 0)
    def _():
        run_ref[...] = jnp.zeros_like(run_ref)

    eid = lax.broadcasted_iota(I32, (N_EXPERTS, tr), 0)
    onehots = [(eid == topi_ref[k:k + 1, :]).astype(F32) for k in range(TOP_K)]
    tot = onehots[0] + onehots[1] + onehots[2] + onehots[3]
    before = _dot(tot.astype(BF16), tri_ref[...])
    base = run_ref[...] + before
    for k in range(TOP_K):
        rank_ref[k:k + 1, :] = jnp.sum(onehots[k] * base, axis=0, keepdims=True).astype(I32)
    run_ref[...] = run_ref[...] + jnp.sum(tot, axis=1, keepdims=True)
    cnt_ref[...] = run_ref[...]


def _rank(topi, tr):
    t = topi.shape[1]
    tri = (jnp.arange(tr)[:, None] < jnp.arange(tr)[None, :]).astype(BF16)
    return pl.pallas_call(
        _rank_kernel,
        grid=(t // tr,),
        in_specs=[pl.BlockSpec((TOP_K, tr), lambda i: (0, i)), pl.BlockSpec((tr, tr), lambda i: (0, 0))],
        out_specs=[pl.BlockSpec((TOP_K, tr), lambda i: (0, i)), pl.BlockSpec((N_EXPERTS, 1), lambda i: (0, 0))],
        out_shape=[jax.ShapeDtypeStruct((TOP_K, t), I32), jax.ShapeDtypeStruct((N_EXPERTS, 1), F32)],
        scratch_shapes=[pltpu.VMEM((N_EXPERTS, 1), F32)],
        compiler_params=_cparams(("arbitrary",)),
        name="rank",
    )(topi, tri)


def _dest_kernel(n_blocks, topi_ref, rank_ref, cnt_ref, dest_ref, blke_ref):
    td = topi_ref.shape[1]
    cnt = cnt_ref[...]
    padded = jnp.floor((cnt + (EXPERT_ROWS - 1)) * (1.0 / EXPERT_ROWS)) * EXPERT_ROWS
    er = lax.broadcasted_iota(I32, (N_EXPERTS, N_EXPERTS), 0)
    ec = lax.broadcasted_iota(I32, (N_EXPERTS, N_EXPERTS), 1)
    pad_row = jnp.sum(jnp.where(er == ec, padded, 0.0), axis=0, keepdims=True)
    start = jnp.sum(jnp.where(ec < er, pad_row, 0.0), axis=1, keepdims=True)
    end = start + padded
    eid = lax.broadcasted_iota(I32, (N_EXPERTS, td), 0)
    for k in range(TOP_K):
        oh = eid == topi_ref[k:k + 1, :]
        s_k = jnp.sum(jnp.where(oh, start, 0.0), axis=0, keepdims=True)
        dest_ref[0, k:k + 1, :] = s_k.astype(I32) + rank_ref[k:k + 1, :]
    bstart = (lax.broadcasted_iota(I32, (N_EXPERTS, n_blocks), 1) * EXPERT_ROWS).astype(F32)
    be = jnp.sum((end <= bstart).astype(I32), axis=0, keepdims=True)
    blke_ref[...] = jnp.minimum(be, N_EXPERTS - 1)


def _dest(topi, rank, cnt, td, n_blocks):
    t = topi.shape[1]
    return pl.pallas_call(
        functools.partial(_dest_kernel, n_blocks),
        grid=(t // td,),
        in_specs=[pl.BlockSpec((TOP_K, td), lambda i: (0, i)), pl.BlockSpec((TOP_K, td), lambda i: (0, i)),
                  pl.BlockSpec((N_EXPERTS, 1), lambda i: (0, 0))],
        out_specs=[pl.BlockSpec((1, TOP_K, td), lambda i: (i, 0, 0)),
                   pl.BlockSpec((1, n_blocks), lambda i: (0, 0))],
        out_shape=[jax.ShapeDtypeStruct((t // td, TOP_K, td), I32), jax.ShapeDtypeStruct((1, n_blocks), I32)],
        compiler_params=_cparams(("arbitrary",)),
        name="dest",
    )(topi, rank, cnt)


def _dispatch_kernel(dest_hbm, u2_hbm, xs_in, xs_hbm, dest_smem, sem_idx, sem):
    del xs_in
    i = pl.program_id(0)
    td = dest_smem.shape[1]
    cp = pltpu.make_async_copy(dest_hbm.at[i], dest_smem, sem_idx)
    cp.start()
    cp.wait()

    def row_copy(t, k):
        return pltpu.make_async_copy(u2_hbm.at[pl.ds(i * td + t, 1)],
                                     xs_hbm.at[pl.ds(dest_smem[k, t], 1)], sem)

    def issue(t, carry):
        for k in range(TOP_K):
            row_copy(t, k).start()
        return carry

    lax.fori_loop(0, td, issue, 0)

    def drain(t, carry):
        for k in range(TOP_K):
            row_copy(t, k).wait()
        return carry

    lax.fori_loop(0, td, drain, 0)


def _dispatch(dest, u2p, n_rows):
    n_tiles, _, td = dest.shape
    xs0 = jnp.zeros((n_rows, u2p.shape[1]), U32)
    return pl.pallas_call(
        _dispatch_kernel,
        grid=(n_tiles,),
        in_specs=[pl.BlockSpec(memory_space=pl.ANY), pl.BlockSpec(memory_space=pl.ANY),
                  pl.BlockSpec(memory_space=pl.ANY)],
        out_specs=pl.BlockSpec(memory_space=pl.ANY),
        out_shape=jax.ShapeDtypeStruct(xs0.shape, U32),
        scratch_shapes=[pltpu.SMEM((TOP_K, td), I32), pltpu.SemaphoreType.DMA, pltpu.SemaphoreType.DMA],
        input_output_aliases={2: 0},
        compiler_params=_cparams(("arbitrary",)),
        name="dispatch",
    )(dest, u2p, xs0)


def _expert_kernel(blke_ref, xs_ref, wug_ref, bug_ref, wd_ref, bd_ref, ys_ref):
    del blke_ref
    f = wd_ref.shape[1]
    x = _unpack_bf16_pairs(xs_ref[...]).astype(BF16)
    h = _dot(x, wug_ref[0]) + bug_ref[0]
    a = jnp.minimum(h[:, :f], SWIGLU_LIMIT)
    lin = jnp.clip(h[:, f:], -SWIGLU_LIMIT, SWIGLU_LIMIT)
    act = (a * _sigmoid(SWIGLU_ALPHA * a) * (lin + 1.0)).astype(BF16)
    y = _dot(act, wd_ref[0]) + bd_ref[0]
    ys_ref[...] = _pack_bf16_pairs(y)


def _experts(blke, xs, w_up_gate, b_up_gate, w_down, b_down):
    n_rows, half = xs.shape
    e, d, f2 = w_up_gate.shape
    n_blocks = n_rows // EXPERT_ROWS
    return pl.pallas_call(
        _expert_kernel,
        grid_spec=pltpu.PrefetchScalarGridSpec(
            num_scalar_prefetch=1,
            grid=(n_blocks,),
            in_specs=[
                pl.BlockSpec((EXPERT_ROWS, half), lambda i, be: (i, 0)),
                pl.BlockSpec((1, d, f2), lambda i, be: (be[i], 0, 0)),
                pl.BlockSpec((1, 1, f2), lambda i, be: (be[i], 0, 0)),
                pl.BlockSpec((1, f2 // 2, d), lambda i, be: (be[i], 0, 0)),
                pl.BlockSpec((1, 1, d), lambda i, be: (be[i], 0, 0)),
            ],
            out_specs=pl.BlockSpec((EXPERT_ROWS, half), lambda i, be: (i, 0)),
        ),
        out_shape=jax.ShapeDtypeStruct((n_rows, half), U32),
        compiler_params=_cparams(("arbitrary",)),
        name="experts",
    )(blke, xs, w_up_gate.astype(BF16), b_up_gate.reshape(e, 1, f2), w_down.astype(BF16),
      b_down.reshape(e, 1, d))


def _combine_kernel(dest_hbm, ys_hbm, h1_ref, topg_ref, o_ref, dest_smem, buf_ref, sem_idx, sem):
    i = pl.program_id(0)
    tc = dest_smem.shape[1]
    cp = pltpu.make_async_copy(dest_hbm.at[i], dest_smem, sem_idx)
    cp.start()
    cp.wait()

    def row_copy(t, k):
        return pltpu.make_async_copy(ys_hbm.at[pl.ds(dest_smem[k, t], 1)], buf_ref.at[k, pl.ds(t, 1)], sem)

    def issue(t, carry):
        for k in range(TOP_K):
            row_copy(t, k).start()
        return carry

    lax.fori_loop(0, tc, issue, 0)

    def drain(t, carry):
        for k in range(TOP_K):
            row_copy(t, k).wait()
        return carry

    lax.fori_loop(0, tc, drain, 0)

    g = jnp.concatenate([topg_ref[...], jnp.zeros((8 - TOP_K, tc), F32)], axis=0).T
    out = h1_ref[...]
    for k in range(TOP_K):
        out = out + g[:, k:k + 1] * _unpack_bf16_pairs(buf_ref[k])
    o_ref[...] = out


def _combine(dest, ys, h1, topg):
    n_tiles, _, tc = dest.shape
    t, d = h1.shape
    return pl.pallas_call(
        _combine_kernel,
        grid=(n_tiles,),
        in_specs=[pl.BlockSpec(memory_space=pl.ANY), pl.BlockSpec(memory_space=pl.ANY),
                  pl.BlockSpec((tc, d), lambda i: (i, 0)), pl.BlockSpec((TOP_K, tc), lambda i: (0, i))],
        out_specs=pl.BlockSpec((tc, d), lambda i: (i, 0)),
        out_shape=jax.ShapeDtypeStruct((t, d), F32),
        scratch_shapes=[pltpu.SMEM((TOP_K, tc), I32), pltpu.VMEM((TOP_K, tc, d // 2), U32),
                        pltpu.SemaphoreType.DMA, pltpu.SemaphoreType.DMA],
        compiler_params=_cparams(("arbitrary",)),
        name="combine",
    )(dest, ys, h1, topg)


def _tile(n, pref):
    return pref if n % pref == 0 else n


def _layer(h, norm1_g, w_in, b_gates, conv_dw, conv_dw_b, conv_ln_g, conv_ln_b, w_conv_out,
           q_norm_g, k_norm_g, w_attn_out, w_out, norm2_g, w_router, b_router,
           w_up_gate, b_up_gate, w_down, b_down):
    b, s, d = h.shape
    t = b * s
    assert s % MOBA_BLOCK == 0 and s // MOBA_BLOCK <= GATE_SLOTS
    tm = _tile(s, 512)
    x2 = h.reshape(t, d)
    hglu, q, k, v, gates, gt = _inproj(x2, norm1_g, w_in, b_gates, q_norm_g, k_norm_g, s, tm)
    c = hglu.shape[1]
    aw = q.shape[1]
    conv = _conv(hglu.reshape(b, s, c), conv_dw, conv_dw_b, conv_ln_g, conv_ln_b, tm)
    slopes = 2.0 ** (-8.0 * jnp.arange(1, N_HEADS + 1, dtype=F32) / N_HEADS)
    attn = _attention(q.reshape(b, s, aw), k.reshape(b, s, aw), v.reshape(b, s, aw), gt, slopes, tm)
    h1, u2p, topi, topg = _mixout(x2, conv.reshape(t, c), attn.reshape(t, aw), gates, w_conv_out, w_attn_out,
                                  w_out, norm2_g, w_router, b_router, tm)
    n_blocks = -(-(t * TOP_K) // EXPERT_ROWS) + N_EXPERTS
    rank, cnt = _rank(topi, tm)
    dest, blke = _dest(topi, rank, cnt, _tile(t, 256), n_blocks)
    xs = _dispatch(dest, u2p, n_blocks * EXPERT_ROWS)
    ys = _experts(blke.reshape(n_blocks), xs, w_up_gate, b_up_gate, w_down, b_down)
    out = _combine(dest, ys, h1, topg)
    return out.reshape(b, s, d)


def kernel(x, norm1_g, w_in, b_gates, conv_dw, conv_dw_b, conv_ln_g, conv_ln_b, w_conv_out, q_norm_g, k_norm_g,
           w_attn_out, w_out, norm2_g, w_router, b_router, w_up_gate, b_up_gate, w_down, b_down):
    h = x
    for l in range(norm1_g.shape[0]):
        h = _layer(h, norm1_g[l], w_in[l], b_gates[l], conv_dw[l], conv_dw_b[l], conv_ln_g[l], conv_ln_b[l],
                   w_conv_out[l], q_norm_g[l], k_norm_g[l], w_attn_out[l], w_out[l], norm2_g[l], w_router[l],
                   b_router[l], w_up_gate[l], b_up_gate[l], w_down[l], b_down[l])
    return h
```

```python
import functools

import jax
import jax.numpy as jnp
from jax import lax
from jax.experimental import pallas as pl
from jax.experimental.pallas import tpu as pltpu

F32 = jnp.float32
BF16 = jnp.bfloat16
I32 = jnp.int32
U32 = jnp.uint32

EPS = 1e-6
CONV_WIDTH = 31
HALO = 32
N_HEADS = 8
HEAD_DIM = 64
MOBA_BLOCK = 256
MOBA_TOPK = 3
GATE_SLOTS = 16
HEADS_PER_GROUP = 4
N_EXPERTS = 32
TOP_K = 4
SWIGLU_LIMIT = 7.0
SWIGLU_ALPHA = 1.702
EXPERT_ROWS = 512
ROW_CHUNKS = 4
DMA_UNROLL = 8
NEG_INF = float("-inf")
VMEM_LIMIT = 56 * 1024 * 1024


def _cparams(sem):
    return pltpu.CompilerParams(dimension_semantics=sem, vmem_limit_bytes=VMEM_LIMIT)


def _dot(a, b):
    return jnp.dot(a, b, preferred_element_type=F32)


def _dot_nt(a, b):
    return lax.dot_general(a, b, (((1,), (1,)), ((), ())), preferred_element_type=F32)


def _split_bf16(x):
    hi = x.astype(BF16)
    lo = (x - hi.astype(F32)).astype(BF16)
    return hi, lo


def _sigmoid(x):
    return 1.0 / (1.0 + jnp.exp(-x))


def _inproj_kernel(tiles_per_seq, x_ref, g1_ref, wglu_ref, wq_ref, wk_ref, wv_ref, wg_ref, bg_ref,
                   qg_ref, kg_ref, bd_ref,
                   h_ref, q_ref, k_ref, v_ref, gate_ref, gt_ref, km_ref):
    i = pl.program_id(0)
    tm = x_ref.shape[0]
    c = h_ref.shape[1]
    aw = q_ref.shape[1]

    @pl.when(i == 0)
    def _():
        km_ref[...] = jnp.zeros_like(km_ref)

    x = x_ref[...]
    ms = jnp.mean(x * x, axis=-1, keepdims=True)
    u = (x * lax.rsqrt(ms + EPS) * g1_ref[...]).astype(BF16)

    glu = _dot(u, wglu_ref[...])
    h_ref[...] = (glu[:, :c] * _sigmoid(glu[:, c:])).astype(BF16)

    bd = bd_ref[...]

    def head_norm(t, g):
        hi, lo = _split_bf16(t * t)
        ss = _dot(hi, bd) + _dot(lo, bd)
        return t * lax.rsqrt(ss * (1.0 / HEAD_DIM) + EPS) * g

    qn = head_norm(_dot(u, wq_ref[...]), qg_ref[...]) * (HEAD_DIM ** -0.5)
    kn = head_norm(_dot(u, wk_ref[...]), kg_ref[...])
    q_hi, q_lo = _split_bf16(qn)
    q_ref[...] = q_hi
    k_ref[...] = kn.astype(BF16)
    v_ref[...] = _dot(u, wv_ref[...]).astype(BF16)
    gate_ref[...] = _sigmoid(_dot(u, wg_ref[...]) + bg_ref[...]).astype(BF16)

    lane_head = lax.broadcasted_iota(I32, (1, aw), 1) // HEAD_DIM
    blocks_per_tile = tm // MOBA_BLOCK
    n0 = (i % tiles_per_seq) * blocks_per_tile
    for j in range(blocks_per_tile):
        kmj = jnp.mean(kn[j * MOBA_BLOCK:(j + 1) * MOBA_BLOCK], axis=0, keepdims=True)
        for h in range(N_HEADS):
            km_ref[pl.ds(h * GATE_SLOTS + n0 + j, 1), :] = jnp.where(lane_head == h, kmj, 0.0)

    km_hi, km_lo = _split_bf16(km_ref[...])
    gt_ref[0] = _dot_nt(km_hi, q_hi) + _dot_nt(km_lo, q_hi) + _dot_nt(km_hi, q_lo)


def _inproj(x2, norm1_g, w_in, b_gates, q_norm_g, k_norm_g, seq, tm):
    t, d = x2.shape
    aw = N_HEADS * HEAD_DIM
    c = (w_in.shape[1] - 3 * aw - 2 * d) // 2
    c1, c2, c3, c4 = 2 * c, 2 * c + aw, 2 * c + 2 * aw, 2 * c + 3 * aw
    wb = w_in.astype(BF16)
    hd = jnp.arange(aw, dtype=I32) // HEAD_DIM
    bd = (hd[:, None] == hd[None, :]).astype(BF16)
    n_tiles = t // tm
    full = lambda a: pl.BlockSpec(a.shape, lambda i: (0,) * a.ndim)
    row = lambda n: pl.BlockSpec((tm, n), lambda i: (i, 0))
    args = (x2, norm1_g.reshape(1, d), wb[:, :c1], wb[:, c1:c2], wb[:, c2:c3], wb[:, c3:c4], wb[:, c4:],
            b_gates.reshape(1, 2 * d), jnp.tile(q_norm_g, N_HEADS).reshape(1, aw),
            jnp.tile(k_norm_g, N_HEADS).reshape(1, aw), bd)
    return pl.pallas_call(
        functools.partial(_inproj_kernel, seq // tm),
        grid=(n_tiles,),
        in_specs=[row(d)] + [full(a) for a in args[1:]],
        out_specs=[row(c), row(aw), row(aw), row(aw), row(2 * d),
                   pl.BlockSpec((1, N_HEADS * GATE_SLOTS, tm), lambda i: (i, 0, 0))],
        out_shape=[jax.ShapeDtypeStruct((t, c), BF16), jax.ShapeDtypeStruct((t, aw), BF16),
                   jax.ShapeDtypeStruct((t, aw), BF16), jax.ShapeDtypeStruct((t, aw), BF16),
                   jax.ShapeDtypeStruct((t, 2 * d), BF16),
                   jax.ShapeDtypeStruct((n_tiles, N_HEADS * GATE_SLOTS, tm), F32)],
        scratch_shapes=[pltpu.VMEM((N_HEADS * GATE_SLOTS, aw), F32)],
        compiler_params=_cparams(("arbitrary",)),
        name="inproj",
    )(*args)


def _conv_kernel(h_ref, dw_ref, dwb_ref, lng_ref, lnb_ref, o_ref, win_ref):
    s = pl.program_id(1)
    ts = h_ref.shape[1]

    @pl.when(s == 0)
    def _():
        win_ref[0:HALO, :] = jnp.zeros((HALO, win_ref.shape[1]), F32)

    @pl.when(s != 0)
    def _():
        win_ref[0:HALO, :] = win_ref[ts:ts + HALO, :]

    win_ref[HALO:HALO + ts, :] = h_ref[0].astype(F32)
    acc = jnp.zeros((ts, win_ref.shape[1]), F32) + dwb_ref[...]
    for w in range(CONV_WIDTH):
        off = HALO - (CONV_WIDTH - 1) + w
        acc = acc + dw_ref[w:w + 1, :] * win_ref[off:off + ts, :]
    mu = jnp.mean(acc, axis=-1, keepdims=True)
    cen = acc - mu
    var = jnp.mean(cen * cen, axis=-1, keepdims=True)
    y = cen * lax.rsqrt(var + EPS) * lng_ref[...] + lnb_ref[...]
    o_ref[0] = (y * _sigmoid(y)).astype(BF16)


def _conv(h3, dw, dwb, lng, lnb, ts):
    b, s, c = h3.shape
    vec = lambda a: pl.BlockSpec(a.shape, lambda i, j: (0, 0))
    args = (h3, dw, dwb.reshape(1, c), lng.reshape(1, c), lnb.reshape(1, c))
    return pl.pallas_call(
        _conv_kernel,
        grid=(b, s // ts),
        in_specs=[pl.BlockSpec((1, ts, c), lambda i, j: (i, j, 0))] + [vec(a) for a in args[1:]],
        out_specs=pl.BlockSpec((1, ts, c), lambda i, j: (i, j, 0)),
        out_shape=jax.ShapeDtypeStruct((b, s, c), BF16),
        scratch_shapes=[pltpu.VMEM((ts + HALO, c), F32)],
        compiler_params=_cparams(("arbitrary", "arbitrary")),
        name="conv",
    )(*args)


def _attn_kernel(q_ref, k_ref, v_ref, gt_ref, slope_ref, o_ref,
                 vt_ref, qbd_ref, alibi_ref, own_ref, mrow_ref, m_ref, l_ref, acc_ref):
    grp = pl.program_id(1)
    qb = pl.program_id(2)
    blk = MOBA_BLOCK
    hp = HEADS_PER_GROUP
    gw = hp * HEAD_DIM
    n_blocks = k_ref.shape[1] // blk
    row_head = lax.broadcasted_iota(I32, (gw, blk), 0) // HEAD_DIM

    @pl.when(qb == 0)
    def _():
        for n in range(n_blocks):
            vt = v_ref[0, n * blk:(n + 1) * blk, :].astype(F32).T.astype(BF16)
            for h in range(hp):
                vt_ref[n, :, h * blk:(h + 1) * blk] = jnp.where(row_head == h, vt, jnp.zeros_like(vt))
        kpos = lax.broadcasted_iota(I32, (blk, blk), 0)
        qpos = lax.broadcasted_iota(I32, (blk, blk), 1)
        for h in range(hp):
            a = kpos.astype(F32) * slope_ref[grp * hp + h]
            alibi_ref[:, h * blk:(h + 1) * blk] = a
            own_ref[:, h * blk:(h + 1) * blk] = jnp.where(kpos <= qpos, a, NEG_INF)

    qt = q_ref[0].astype(F32).T.astype(BF16)
    for h in range(hp):
        qbd_ref[:, h * blk:(h + 1) * blk] = jnp.where(row_head == h, qt, jnp.zeros_like(qt))

    slot = lax.broadcasted_iota(I32, (GATE_SLOTS, blk), 0)
    for h in range(hp):
        g = gt_ref[0, h * GATE_SLOTS:(h + 1) * GATE_SLOTS, :]
        g = jnp.where(slot < qb, g, NEG_INF)
        sel = jnp.zeros((GATE_SLOTS, blk), jnp.bool_)
        for _ in range(MOBA_TOPK):
            mx = jnp.max(g, axis=0, keepdims=True)
            first = jnp.min(jnp.where(g == mx, slot, GATE_SLOTS), axis=0, keepdims=True)
            pick = (slot == first) & (mx > NEG_INF)
            sel = sel | pick
            g = jnp.where(pick, NEG_INF, g)
        dist = ((qb - slot) * blk).astype(F32) * slope_ref[grp * hp + h]
        mrow_ref[:, h * blk:(h + 1) * blk] = jnp.where(sel, -dist, NEG_INF)

    def scores(n):
        kb = k_ref[0, pl.ds(pl.multiple_of(n * blk, blk), blk), :]
        return _dot(kb, qbd_ref[...])

    def pv(n, p):
        pcat = jnp.concatenate([p[:, h * blk:(h + 1) * blk] for h in range(hp)], axis=0).astype(BF16)
        return _dot(vt_ref[n], pcat)

    def expand(r):
        return jnp.concatenate(
            [jnp.broadcast_to(r[:, h * blk:(h + 1) * blk], (HEAD_DIM, blk)) for h in range(hp)], axis=0)

    t = scores(qb) + own_ref[...]
    m0 = jnp.max(t, axis=0, keepdims=True)
    p = jnp.exp(t - m0)
    m_ref[...] = m0
    l_ref[...] = jnp.sum(p, axis=0, keepdims=True)
    acc_ref[...] = pv(qb, p)

    def body(n, carry):
        t = scores(n) + alibi_ref[...] + mrow_ref[pl.ds(n, 1), :]
        m_old = m_ref[...]
        m_new = jnp.maximum(m_old, jnp.max(t, axis=0, keepdims=True))
        alpha = jnp.exp(m_old - m_new)
        p = jnp.exp(t - m_new)
        m_ref[...] = m_new
        l_ref[...] = alpha * l_ref[...] + jnp.sum(p, axis=0, keepdims=True)
        acc_ref[...] = expand(alpha) * acc_ref[...] + pv(n, p)
        return carry

    lax.fori_loop(0, qb, body, 0)
    out = acc_ref[...] / expand(l_ref[...])
    o_ref[0] = out.T.astype(BF16)


def _attention(q3, k3, v3, gt, slopes, tm):
    b, s, aw = q3.shape
    blk = MOBA_BLOCK
    gw = HEADS_PER_GROUP * HEAD_DIM
    n_groups = aw // gw
    nb = s // blk
    per_tile = tm // blk
    tiles_per_seq = s // tm
    return pl.pallas_call(
        _attn_kernel,
        grid_spec=pltpu.PrefetchScalarGridSpec(
            num_scalar_prefetch=0,
            grid=(b, n_groups, nb),
            in_specs=[
                pl.BlockSpec((1, blk, gw), lambda i, g, j: (i, j, g)),
                pl.BlockSpec((1, s, gw), lambda i, g, j: (i, 0, g)),
                pl.BlockSpec((1, s, gw), lambda i, g, j: (i, 0, g)),
                pl.BlockSpec((1, HEADS_PER_GROUP * GATE_SLOTS, blk),
                             lambda i, g, j: (i * tiles_per_seq + j // per_tile, g, j % per_tile)),
                pl.BlockSpec(memory_space=pltpu.SMEM),
            ],
            out_specs=pl.BlockSpec((1, blk, gw), lambda i, g, j: (i, j, g)),
            scratch_shapes=[
                pltpu.VMEM((nb, gw, HEADS_PER_GROUP * blk), BF16),
                pltpu.VMEM((gw, HEADS_PER_GROUP * blk), BF16),
                pltpu.VMEM((blk, HEADS_PER_GROUP * blk), F32),
                pltpu.VMEM((blk, HEADS_PER_GROUP * blk), F32),
                pltpu.VMEM((GATE_SLOTS, HEADS_PER_GROUP * blk), F32),
                pltpu.VMEM((1, HEADS_PER_GROUP * blk), F32),
                pltpu.VMEM((1, HEADS_PER_GROUP * blk), F32),
                pltpu.VMEM((gw, blk), F32),
            ],
        ),
        out_shape=jax.ShapeDtypeStruct((b, s, aw), BF16),
        compiler_params=_cparams(("arbitrary", "arbitrary", "arbitrary")),
        name="attn",
    )(q3, k3, v3, gt, slopes)


def _pack_bf16_pairs(x):
    n = x.shape[1] // 2
    r = x.astype(BF16).astype(F32)
    lo = lax.shift_right_logical(pltpu.bitcast(r[:, :n], U32), jnp.uint32(16))
    hi = pltpu.bitcast(r[:, n:], U32) & jnp.uint32(0xFFFF0000)
    return hi | lo


def _store_rows(ref, x):
    pk = _pack_bf16_pairs(x)
    m, n = pk.shape
    c = n // 128
    for j in range(c):
        ref[pl.ds(j, m, stride=c), :] = pk[:, j * 128:(j + 1) * 128]


def _load_rows(ref, m):
    c = ref.shape[0] // m
    ws = [ref[pl.ds(j, m, stride=c), :] for j in range(c)]
    lo = [pltpu.bitcast(lax.shift_left(w, jnp.uint32(16)), F32) for w in ws]
    hi = [pltpu.bitcast(w & jnp.uint32(0xFFFF0000), F32) for w in ws]
    return jnp.concatenate(lo + hi, axis=1)


def _mixout_kernel(x_ref, c_ref, a_ref, gate_ref, wco_ref, wao_ref, wout_ref, g2_ref,
                   wrh_ref, wrl_ref, br_ref,
                   h1_ref, u2_ref, topi_ref, topg_ref):
    d = x_ref.shape[1]
    y_c = _dot(c_ref[...], wco_ref[...])
    y_a = _dot(a_ref[...], wao_ref[...])
    gate = gate_ref[...].astype(F32)
    mixed = (gate[:, :d] * y_c + gate[:, d:] * y_a).astype(BF16)
    h1 = x_ref[...] + _dot(mixed, wout_ref[...])
    h1_ref[...] = h1
    ms = jnp.mean(h1 * h1, axis=-1, keepdims=True)
    u2 = h1 * lax.rsqrt(ms + EPS) * g2_ref[...]
    _store_rows(u2_ref, u2)
    u_hi, u_lo = _split_bf16(u2)
    wrh = wrh_ref[...]
    logits = _dot_nt(wrh, u_hi) + _dot_nt(wrl_ref[...], u_hi) + _dot_nt(wrh, u_lo) + br_ref[...]
    eid = lax.broadcasted_iota(I32, logits.shape, 0)
    vals = []
    for k in range(TOP_K):
        mx = jnp.max(logits, axis=0, keepdims=True)
        first = jnp.min(jnp.where(logits == mx, eid, N_EXPERTS), axis=0, keepdims=True)
        topi_ref[k:k + 1, :] = first
        vals.append(mx)
        logits = jnp.where(eid == first, NEG_INF, logits)
    es = [jnp.exp(v - vals[0]) for v in vals]
    den = es[0] + es[1] + es[2] + es[3]
    for k in range(TOP_K):
        topg_ref[k:k + 1, :] = es[k] / den


def _mixout(x2, c2, a2, gates, w_conv_out, w_attn_out, w_out, norm2_g, w_router, b_router, tm):
    t, d = x2.shape
    wr_t = w_router.T
    wrh = wr_t.astype(BF16)
    wrl = (wr_t - wrh.astype(F32)).astype(BF16)
    args = (x2, c2, a2, gates, w_conv_out.astype(BF16), w_attn_out.astype(BF16), w_out.astype(BF16),
            norm2_g.reshape(1, d), wrh, wrl, b_router.reshape(N_EXPERTS, 1))
    full = lambda a: pl.BlockSpec(a.shape, lambda i: (0,) * a.ndim)
    row = lambda a: pl.BlockSpec((tm, a.shape[1]), lambda i: (i, 0))
    return pl.pallas_call(
        _mixout_kernel,
        grid=(t // tm,),
        in_specs=[row(a) for a in args[:4]] + [full(a) for a in args[4:]],
        out_specs=[pl.BlockSpec((tm, d), lambda i: (i, 0)), pl.BlockSpec((tm * ROW_CHUNKS, 128), lambda i: (i, 0)),
                   pl.BlockSpec((TOP_K, tm), lambda i: (0, i)), pl.BlockSpec((TOP_K, tm), lambda i: (0, i))],
        out_shape=[jax.ShapeDtypeStruct((t, d), F32), jax.ShapeDtypeStruct((t * ROW_CHUNKS, 128), U32),
                   jax.ShapeDtypeStruct((TOP_K, t), I32), jax.ShapeDtypeStruct((TOP_K, t), F32)],
        compiler_params=_cparams(("arbitrary",)),
        name="mixout",
    )(*args)


def _rank_kernel(topi_ref, tri_ref, rank_ref, cnt_ref, run_ref):
    i = pl.program_id(0)
    tr = topi_ref.shape[1]

    @pl.when(i == 0)
    def _():
        run_ref[...] = jnp.zeros_like(run_ref)

    eid = lax.broadcasted_iota(I32, (N_EXPERTS, tr), 0)
    onehots = [(eid == topi_ref[k:k + 1, :]).astype(F32) for k in range(TOP_K)]
    tot = onehots[0] + onehots[1] + onehots[2] + onehots[3]
    before = _dot(tot.astype(BF16), tri_ref[...])
    base = run_ref[...] + before
    for k in range(TOP_K):
        rank_ref[k:k + 1, :] = jnp.sum(onehots[k] * base, axis=0, keepdims=True).astype(I32)
    run_ref[...] = run_ref[...] + jnp.sum(tot, axis=1, keepdims=True)
    cnt_ref[...] = run_ref[...]


def _rank(topi, tr):
    t = topi.shape[1]
    tri = (jnp.arange(tr)[:, None] < jnp.arange(tr)[None, :]).astype(BF16)
    return pl.pallas_call(
        _rank_kernel,
        grid=(t // tr,),
        in_specs=[pl.BlockSpec((TOP_K, tr), lambda i: (0, i)), pl.BlockSpec((tr, tr), lambda i: (0, 0))],
        out_specs=[pl.BlockSpec((TOP_K, tr), lambda i: (0, i)), pl.BlockSpec((N_EXPERTS, 1), lambda i: (0, 0))],
        out_shape=[jax.ShapeDtypeStruct((TOP_K, t), I32), jax.ShapeDtypeStruct((N_EXPERTS, 1), F32)],
        scratch_shapes=[pltpu.VMEM((N_EXPERTS, 1), F32)],
        compiler_params=_cparams(("arbitrary",)),
        name="rank",
    )(topi, tri)


def _dest_kernel(n_blocks, topi_ref, rank_ref, cnt_ref, dest_ref, blke_ref):
    td = topi_ref.shape[1]
    cnt = cnt_ref[...]
    padded = jnp.floor((cnt + (EXPERT_ROWS - 1)) * (1.0 / EXPERT_ROWS)) * EXPERT_ROWS
    er = lax.broadcasted_iota(I32, (N_EXPERTS, N_EXPERTS), 0)
    ec = lax.broadcasted_iota(I32, (N_EXPERTS, N_EXPERTS), 1)
    pad_row = jnp.sum(jnp.where(er == ec, padded, 0.0), axis=0, keepdims=True)
    start = jnp.sum(jnp.where(ec < er, pad_row, 0.0), axis=1, keepdims=True)
    end = start + padded
    eid = lax.broadcasted_iota(I32, (N_EXPERTS, td), 0)
    for k in range(TOP_K):
        oh = eid == topi_ref[k:k + 1, :]
        s_k = jnp.sum(jnp.where(oh, start, 0.0), axis=0, keepdims=True)
        dest_ref[0, k:k + 1, :] = (s_k.astype(I32) + rank_ref[k:k + 1, :]) * ROW_CHUNKS
    bstart = (lax.broadcasted_iota(I32, (N_EXPERTS, n_blocks), 1) * EXPERT_ROWS).astype(F32)
    be = jnp.sum((end <= bstart).astype(I32), axis=0, keepdims=True)
    blke_ref[...] = jnp.minimum(be, N_EXPERTS - 1)


def _dest(topi, rank, cnt, td, n_blocks):
    t = topi.shape[1]
    return pl.pallas_call(
        functools.partial(_dest_kernel, n_blocks),
        grid=(t // td,),
        in_specs=[pl.BlockSpec((TOP_K, td), lambda i: (0, i)), pl.BlockSpec((TOP_K, td), lambda i: (0, i)),
                  pl.BlockSpec((N_EXPERTS, 1), lambda i: (0, 0))],
        out_specs=[pl.BlockSpec((1, TOP_K, td), lambda i: (i, 0, 0)),
                   pl.BlockSpec((1, n_blocks), lambda i: (0, 0))],
        out_shape=[jax.ShapeDtypeStruct((t // td, TOP_K, td), I32), jax.ShapeDtypeStruct((1, n_blocks), I32)],
        compiler_params=_cparams(("arbitrary",)),
        name="dest",
    )(topi, rank, cnt)


def _row_dma_all(n_tok, row_copy):
    def issue(it, carry):
        for u in range(DMA_UNROLL):
            for k in range(TOP_K):
                row_copy(it * DMA_UNROLL + u, k).start()
        return carry

    lax.fori_loop(0, n_tok // DMA_UNROLL, issue, 0)

    def drain(it, carry):
        for u in range(DMA_UNROLL):
            for k in range(TOP_K):
                row_copy(it * DMA_UNROLL + u, k).wait()
        return carry

    lax.fori_loop(0, n_tok // DMA_UNROLL, drain, 0)


def _load_dest(dest_hbm, dest_smem, sem_idx):
    cp = pltpu.make_async_copy(dest_hbm.at[pl.program_id(0)], dest_smem, sem_idx)
    cp.start()
    cp.wait()


def _dispatch_kernel(dest_hbm, u2_ref, xs_in, xs_hbm, dest_smem, sem_idx, sem):
    del xs_in
    td = u2_ref.shape[0] // ROW_CHUNKS
    _load_dest(dest_hbm, dest_smem, sem_idx)

    def row_copy(t, k):
        src = pl.multiple_of(t * ROW_CHUNKS, ROW_CHUNKS)
        dst = pl.multiple_of(dest_smem[k * td + t], ROW_CHUNKS)
        return pltpu.make_async_copy(u2_ref.at[pl.ds(src, ROW_CHUNKS)], xs_hbm.at[pl.ds(dst, ROW_CHUNKS)], sem)

    _row_dma_all(td, row_copy)


def _dispatch(dest, u2p, n_rows):
    n_tiles, per_tile = dest.shape
    td = per_tile // TOP_K
    xs0 = jnp.zeros((n_rows * ROW_CHUNKS, 128), U32)
    return pl.pallas_call(
        _dispatch_kernel,
        grid=(n_tiles,),
        in_specs=[pl.BlockSpec(memory_space=pl.ANY), pl.BlockSpec((td * ROW_CHUNKS, 128), lambda i: (i, 0)),
                  pl.BlockSpec(memory_space=pl.ANY)],
        out_specs=pl.BlockSpec(memory_space=pl.ANY),
        out_shape=jax.ShapeDtypeStruct(xs0.shape, U32),
        scratch_shapes=[pltpu.SMEM((per_tile,), I32), pltpu.SemaphoreType.DMA, pltpu.SemaphoreType.DMA],
        input_output_aliases={2: 0},
        compiler_params=_cparams(("arbitrary",)),
        name="dispatch",
    )(dest, u2p, xs0)


def _expert_kernel(blke_ref, xs_ref, wug_ref, bug_ref, wd_ref, bd_ref, ys_ref):
    del blke_ref
    f = wd_ref.shape[1]
    x = _load_rows(xs_ref, EXPERT_ROWS).astype(BF16)
    h = _dot(x, wug_ref[0]) + bug_ref[0]
    a = jnp.minimum(h[:, :f], SWIGLU_LIMIT)
    lin = jnp.clip(h[:, f:], -SWIGLU_LIMIT, SWIGLU_LIMIT)
    act = (a * _sigmoid(SWIGLU_ALPHA * a) * (lin + 1.0)).astype(BF16)
    y = _dot(act, wd_ref[0]) + bd_ref[0]
    _store_rows(ys_ref, y)


def _experts(blke, xs, w_up_gate, b_up_gate, w_down, b_down):
    e, d, f2 = w_up_gate.shape
    blk_rows = EXPERT_ROWS * ROW_CHUNKS
    n_blocks = xs.shape[0] // blk_rows
    return pl.pallas_call(
        _expert_kernel,
        grid_spec=pltpu.PrefetchScalarGridSpec(
            num_scalar_prefetch=1,
            grid=(n_blocks,),
            in_specs=[
                pl.BlockSpec((blk_rows, 128), lambda i, be: (i, 0)),
                pl.BlockSpec((1, d, f2), lambda i, be: (be[i], 0, 0)),
                pl.BlockSpec((1, 1, f2), lambda i, be: (be[i], 0, 0)),
                pl.BlockSpec((1, f2 // 2, d), lambda i, be: (be[i], 0, 0)),
                pl.BlockSpec((1, 1, d), lambda i, be: (be[i], 0, 0)),
            ],
            out_specs=pl.BlockSpec((blk_rows, 128), lambda i, be: (i, 0)),
        ),
        out_shape=jax.ShapeDtypeStruct(xs.shape, U32),
        compiler_params=_cparams(("arbitrary",)),
        name="experts",
    )(blke, xs, w_up_gate.astype(BF16), b_up_gate.reshape(e, 1, f2), w_down.astype(BF16),
      b_down.reshape(e, 1, d))


def _combine_kernel(dest_hbm, ys_hbm, h1_ref, topg_ref, o_ref, dest_smem, b0, b1, b2, b3, sem_idx, sem):
    tc = h1_ref.shape[0]
    bufs = (b0, b1, b2, b3)
    _load_dest(dest_hbm, dest_smem, sem_idx)

    def row_copy(t, k):
        src = pl.multiple_of(dest_smem[k * tc + t], ROW_CHUNKS)
        dst = pl.multiple_of(t * ROW_CHUNKS, ROW_CHUNKS)
        return pltpu.make_async_copy(ys_hbm.at[pl.ds(src, ROW_CHUNKS)], bufs[k].at[pl.ds(dst, ROW_CHUNKS)], sem)

    _row_dma_all(tc, row_copy)

    g = jnp.concatenate([topg_ref[...], jnp.zeros((8 - TOP_K, tc), F32)], axis=0).T
    out = h1_ref[...]
    for k in range(TOP_K):
        out = out + g[:, k:k + 1] * _load_rows(bufs[k], tc)
    o_ref[...] = out


def _combine(dest, ys, h1, topg):
    n_tiles, per_tile = dest.shape
    tc = per_tile // TOP_K
    t, d = h1.shape
    return pl.pallas_call(
        _combine_kernel,
        grid=(n_tiles,),
        in_specs=[pl.BlockSpec(memory_space=pl.ANY), pl.BlockSpec(memory_space=pl.ANY),
                  pl.BlockSpec((tc, d), lambda i: (i, 0)), pl.BlockSpec((TOP_K, tc), lambda i: (0, i))],
        out_specs=pl.BlockSpec((tc, d), lambda i: (i, 0)),
        out_shape=jax.ShapeDtypeStruct((t, d), F32),
        scratch_shapes=[pltpu.SMEM((per_tile,), I32)] + [pltpu.VMEM((tc * ROW_CHUNKS, 128), U32)] * TOP_K
                       + [pltpu.SemaphoreType.DMA, pltpu.SemaphoreType.DMA],
        compiler_params=_cparams(("arbitrary",)),
        name="combine",
    )(dest, ys, h1, topg)


def _tile(n, pref):
    return pref if n % pref == 0 else n


def _layer(h, norm1_g, w_in, b_gates, conv_dw, conv_dw_b, conv_ln_g, conv_ln_b, w_conv_out,
           q_norm_g, k_norm_g, w_attn_out, w_out, norm2_g, w_router, b_router,
           w_up_gate, b_up_gate, w_down, b_down):
    b, s, d = h.shape
    t = b * s
    assert s % MOBA_BLOCK == 0 and s // MOBA_BLOCK <= GATE_SLOTS
    tm = _tile(s, 512)
    x2 = h.reshape(t, d)
    hglu, q, k, v, gates, gt = _inproj(x2, norm1_g, w_in, b_gates, q_norm_g, k_norm_g, s, tm)
    c = hglu.shape[1]
    aw = q.shape[1]
    conv = _conv(hglu.reshape(b, s, c), conv_dw, conv_dw_b, conv_ln_g, conv_ln_b, tm)
    slopes = 2.0 ** (-8.0 * jnp.arange(1, N_HEADS + 1, dtype=F32) / N_HEADS)
    attn = _attention(q.reshape(b, s, aw), k.reshape(b, s, aw), v.reshape(b, s, aw), gt, slopes, tm)
    h1, u2p, topi, topg = _mixout(x2, conv.reshape(t, c), attn.reshape(t, aw), gates, w_conv_out, w_attn_out,
                                  w_out, norm2_g, w_router, b_router, tm)
    n_blocks = -(-(t * TOP_K) // EXPERT_ROWS) + N_EXPERTS
    rank, cnt = _rank(topi, tm)
    dest, blke = _dest(topi, rank, cnt, tm, n_blocks)
    dest = dest.reshape(t // tm, TOP_K * tm)
    xs = _dispatch(dest, u2p, n_blocks * EXPERT_ROWS)
    ys = _experts(blke.reshape(n_blocks), xs, w_up_gate, b_up_gate, w_down, b_down)
    out = _combine(dest, ys, h1, topg)
    return out.reshape(b, s, d)


def kernel(x, norm1_g, w_in, b_gates, conv_dw, conv_dw_b, conv_ln_g, conv_ln_b, w_conv_out, q_norm_g, k_norm_g,
           w_attn_out, w_out, norm2_g, w_router, b_router, w_up_gate, b_up_gate, w_down, b_down):
    h = x
    for l in range(norm1_g.shape[0]):
        h = _layer(h, norm1_g[l], w_in[l], b_gates[l], conv_dw[l], conv_dw_b[l], conv_ln_g[l], conv_ln_b[l],
                   w_conv_out[l], q_norm_g[l], k_norm_g[l], w_attn_out[l], w_out[l], norm2_g[l], w_router[l],
                   b_router[l], w_up_gate[l], b_up_gate[l], w_down[l], b_down[l])
    return h
```

```python
import functools

import jax
import jax.numpy as jnp
from jax import lax
from jax.experimental import pallas as pl
from jax.experimental.pallas import tpu as pltpu

F32 = jnp.float32
BF16 = jnp.bfloat16
I32 = jnp.int32
U32 = jnp.uint32

EPS = 1e-6
CONV_WIDTH = 31
HALO = 32
N_HEADS = 8
HEAD_DIM = 64
MOBA_BLOCK = 256
MOBA_TOPK = 3
GATE_SLOTS = 16
HEADS_PER_GROUP = 4
K_AUG = 128
AUG_POS = 16
V_ROWS = 80
MASKED = -1e30
LOG2E = 1.4426950408889634
M_INIT = -3e38
ATTN_UNROLL = 4
ATTN_LOOKAHEAD = 4
ATTN_SLOTS = 6
N_EXPERTS = 32
TOP_K = 4
SWIGLU_LIMIT = 7.0
SWIGLU_ALPHA = 1.702
EXPERT_ROWS = 512
ROW_CHUNKS = 4
DMA_UNROLL = 8
NEG_INF = float("-inf")
VMEM_LIMIT = 56 * 1024 * 1024


def _cparams(sem):
    return pltpu.CompilerParams(dimension_semantics=sem, vmem_limit_bytes=VMEM_LIMIT)


def _dot(a, b):
    return jnp.dot(a, b, preferred_element_type=F32)


def _dot_nt(a, b):
    return lax.dot_general(a, b, (((1,), (1,)), ((), ())), preferred_element_type=F32)


def _split_bf16(x):
    hi = x.astype(BF16)
    lo = (x - hi.astype(F32)).astype(BF16)
    return hi, lo


def _sigmoid(x):
    return 1.0 / (1.0 + jnp.exp(-x))


def _inproj_kernel(tiles_per_seq, x_ref, g1_ref, wglu_ref, wq_ref, wk_ref, wv_ref, wg_ref, bg_ref,
                   qg_ref, kg_ref, bd_ref, place_ref,
                   h_ref, q_ref, k_ref, v_ref, gate_ref, gt_ref, km_ref):
    i = pl.program_id(0)
    tm = x_ref.shape[0]
    c = h_ref.shape[1]
    aw = q_ref.shape[1]

    @pl.when(i == 0)
    def _():
        km_ref[...] = jnp.zeros_like(km_ref)

    x = x_ref[...]
    ms = jnp.mean(x * x, axis=-1, keepdims=True)
    u = (x * lax.rsqrt(ms + EPS) * g1_ref[...]).astype(BF16)

    glu = _dot(u, wglu_ref[...])
    h_ref[...] = (glu[:, :c] * _sigmoid(glu[:, c:])).astype(BF16)

    bd = bd_ref[...]

    def head_norm(t, g):
        ss = _dot((t * t).astype(BF16), bd)
        return t * lax.rsqrt(ss * (1.0 / HEAD_DIM) + EPS) * g

    qn = head_norm(_dot(u, wq_ref[...]), qg_ref[...]) * (HEAD_DIM ** -0.5 * LOG2E)
    kn = head_norm(_dot(u, wk_ref[...]), kg_ref[...])
    q_hi, q_lo = _split_bf16(qn)
    q_ref[...] = q_hi
    pos = lax.broadcasted_iota(I32, (tm, K_AUG), 0) + (i % tiles_per_seq) * tm
    col = lax.broadcasted_iota(I32, (tm, K_AUG), 1)
    kpos = pos & (MOBA_BLOCK - 1)
    bstart = pos - kpos
    aug = jnp.where((col >= HEAD_DIM) & (col < HEAD_DIM + 3), kpos,
                    jnp.where((col >= HEAD_DIM + 3) & (col < HEAD_DIM + 6), bstart,
                              jnp.where((col - (HEAD_DIM + AUG_POS)) * MOBA_BLOCK == bstart, 1, 0))).astype(F32)
    kp = _dot(kn.astype(BF16), place_ref[...])
    for h in range(N_HEADS):
        k_ref[:, h * K_AUG:(h + 1) * K_AUG] = (kp[:, h * K_AUG:(h + 1) * K_AUG] + aug).astype(BF16)
    v_ref[...] = _dot(u, wv_ref[...]).astype(BF16)
    gate_ref[...] = _sigmoid(_dot(u, wg_ref[...]) + bg_ref[...]).astype(BF16)

    lane_head = lax.broadcasted_iota(I32, (1, aw), 1) // HEAD_DIM
    blocks_per_tile = tm // MOBA_BLOCK
    n0 = (i % tiles_per_seq) * blocks_per_tile
    for j in range(blocks_per_tile):
        kmj = jnp.mean(kn[j * MOBA_BLOCK:(j + 1) * MOBA_BLOCK], axis=0, keepdims=True)
        for h in range(N_HEADS):
            km_ref[pl.ds(h * GATE_SLOTS + n0 + j, 1), :] = jnp.where(lane_head == h, kmj, 0.0)

    km_hi, km_lo = _split_bf16(km_ref[...])
    gt_ref[0] = _dot_nt(km_hi, q_hi) + _dot_nt(km_lo, q_hi) + _dot_nt(km_hi, q_lo)


def _inproj(x2, norm1_g, w_in, b_gates, q_norm_g, k_norm_g, seq, tm):
    t, d = x2.shape
    aw = N_HEADS * HEAD_DIM
    c = (w_in.shape[1] - 3 * aw - 2 * d) // 2
    c1, c2, c3, c4 = 2 * c, 2 * c + aw, 2 * c + 2 * aw, 2 * c + 3 * aw
    wb = w_in.astype(BF16)
    hd = jnp.arange(aw, dtype=I32) // HEAD_DIM
    bd = (hd[:, None] == hd[None, :]).astype(BF16)
    lane = jnp.arange(aw, dtype=I32)
    place = ((lane // HEAD_DIM * K_AUG + lane % HEAD_DIM)[:, None]
             == jnp.arange(N_HEADS * K_AUG, dtype=I32)[None, :]).astype(BF16)
    n_tiles = t // tm
    full = lambda a: pl.BlockSpec(a.shape, lambda i: (0,) * a.ndim)
    row = lambda n: pl.BlockSpec((tm, n), lambda i: (i, 0))
    args = (x2, norm1_g.reshape(1, d), wb[:, :c1], wb[:, c1:c2], wb[:, c2:c3], wb[:, c3:c4], wb[:, c4:],
            b_gates.reshape(1, 2 * d), jnp.tile(q_norm_g, N_HEADS).reshape(1, aw),
            jnp.tile(k_norm_g, N_HEADS).reshape(1, aw), bd, place)
    return pl.pallas_call(
        functools.partial(_inproj_kernel, seq // tm),
        grid=(n_tiles,),
        in_specs=[row(d)] + [full(a) for a in args[1:]],
        out_specs=[row(c), row(aw), row(N_HEADS * K_AUG), row(aw), row(2 * d),
                   pl.BlockSpec((1, N_HEADS * GATE_SLOTS, tm), lambda i: (i, 0, 0))],
        out_shape=[jax.ShapeDtypeStruct((t, c), BF16), jax.ShapeDtypeStruct((t, aw), BF16),
                   jax.ShapeDtypeStruct((t, N_HEADS * K_AUG), BF16), jax.ShapeDtypeStruct((t, aw), BF16),
                   jax.ShapeDtypeStruct((t, 2 * d), BF16),
                   jax.ShapeDtypeStruct((n_tiles, N_HEADS * GATE_SLOTS, tm), F32)],
        scratch_shapes=[pltpu.VMEM((N_HEADS * GATE_SLOTS, aw), F32)],
        compiler_params=_cparams(("arbitrary",)),
        name="inproj",
    )(*args)


def _conv_kernel(h_ref, dw_ref, dwb_ref, lng_ref, lnb_ref, o_ref, win_ref, sh_ref):
    s = pl.program_id(1)
    ts = h_ref.shape[1]

    @pl.when(s == 0)
    def _():
        win_ref[0:HALO, :] = jnp.zeros((HALO, win_ref.shape[1]), F32)

    @pl.when(s != 0)
    def _():
        win_ref[0:HALO, :] = win_ref[ts:ts + HALO, :]

    win_ref[HALO:HALO + ts, :] = h_ref[0].astype(F32)
    span = ts + HALO - 8
    for r in range(1, 8):
        sh_ref[r - 1] = win_ref[r:r + span, :]
    acc = jnp.zeros((ts, win_ref.shape[1]), F32) + dwb_ref[...]
    for w in range(CONV_WIDTH):
        off = HALO - (CONV_WIDTH - 1) + w
        r = off % 8
        tap = win_ref[off:off + ts, :] if r == 0 else sh_ref[r - 1, off - r:off - r + ts, :]
        acc = acc + dw_ref[w:w + 1, :] * tap
    mu = jnp.mean(acc, axis=-1, keepdims=True)
    cen = acc - mu
    var = jnp.mean(cen * cen, axis=-1, keepdims=True)
    y = cen * lax.rsqrt(var + EPS) * lng_ref[...] + lnb_ref[...]
    o_ref[0] = (y * _sigmoid(y)).astype(BF16)


def _conv(h3, dw, dwb, lng, lnb, ts):
    b, s, c = h3.shape
    vec = lambda a: pl.BlockSpec(a.shape, lambda i, j: (0, 0))
    args = (h3, dw, dwb.reshape(1, c), lng.reshape(1, c), lnb.reshape(1, c))
    return pl.pallas_call(
        _conv_kernel,
        grid=(b, s // ts),
        in_specs=[pl.BlockSpec((1, ts, c), lambda i, j: (i, j, 0))] + [vec(a) for a in args[1:]],
        out_specs=pl.BlockSpec((1, ts, c), lambda i, j: (i, j, 0)),
        out_shape=jax.ShapeDtypeStruct((b, s, c), BF16),
        scratch_shapes=[pltpu.VMEM((ts + HALO, c), F32), pltpu.VMEM((7, ts + HALO - 8, c), F32)],
        compiler_params=_cparams(("arbitrary", "arbitrary")),
        name="conv",
    )(*args)


def _attn_kernel(q_ref, k_ref, v_ref, gt_ref, srow_ref, o_ref,
                 vt_ref, caus_ref, w_ref, s_ref, cm_ref, m_ref, acc_ref):
    qb = pl.program_id(2)
    blk = MOBA_BLOCK
    hp = HEADS_PER_GROUP
    n_blocks = v_ref.shape[1] // blk

    @pl.when(qb == 0)
    def _():
        ones_rows = (lax.broadcasted_iota(I32, (V_ROWS - HEAD_DIM, blk), 0) == 0).astype(BF16)
        for n in range(n_blocks):
            vt = v_ref[0, n * blk:(n + 1) * blk, :].astype(F32).T
            for h in range(hp):
                vt_ref[n, h, 0:HEAD_DIM, :] = vt[h * HEAD_DIM:(h + 1) * HEAD_DIM].astype(BF16)
                vt_ref[n, h, HEAD_DIM:V_ROWS, :] = ones_rows
        kpos = lax.broadcasted_iota(I32, (blk, blk), 0)
        qpos = lax.broadcasted_iota(I32, (blk, blk), 1)
        caus_ref[0] = jnp.zeros((blk, blk), F32)
        caus_ref[1] = jnp.where(kpos <= qpos, 0.0, NEG_INF)
        for h in range(hp):
            w_ref[h, HEAD_DIM:HEAD_DIM + AUG_POS, :] = srow_ref[h]
            w_ref[h, HEAD_DIM + AUG_POS + GATE_SLOTS:K_AUG, :] = jnp.zeros(
                (K_AUG - HEAD_DIM - AUG_POS - GATE_SLOTS, blk), BF16)

    qt = q_ref[0].astype(F32).T
    slot = lax.broadcasted_iota(I32, (GATE_SLOTS, blk), 0)
    for h in range(hp):
        g = gt_ref[0, h * GATE_SLOTS:(h + 1) * GATE_SLOTS, :]
        g = jnp.where(slot < qb, g, NEG_INF)
        sel = slot == qb
        for _ in range(MOBA_TOPK):
            mx = jnp.max(g, axis=0, keepdims=True)
            first = jnp.min(jnp.where(g == mx, slot, GATE_SLOTS), axis=0, keepdims=True)
            pick = (slot == first) & (mx > NEG_INF)
            sel = sel | pick
            g = jnp.where(pick, NEG_INF, g)
        w_ref[h, 0:HEAD_DIM, :] = qt[h * HEAD_DIM:(h + 1) * HEAD_DIM].astype(BF16)
        w_ref[h, HEAD_DIM + AUG_POS:HEAD_DIM + AUG_POS + GATE_SLOTS, :] = jnp.where(sel, 0.0, MASKED).astype(BF16)

    m_ref[...] = jnp.full(m_ref.shape, M_INIT, F32)
    acc_ref[...] = jnp.zeros(acc_ref.shape, F32)

    def scores(n, h, slot):
        start = pl.multiple_of(n * blk, blk)
        t = _dot(k_ref[0, pl.ds(start, blk), h * K_AUG:(h + 1) * K_AUG], w_ref[h])
        t = t + caus_ref[jnp.where(n == qb, 1, 0)]
        s_ref[slot] = t
        cm_ref[slot] = jnp.max(t, axis=0, keepdims=True)

    def accumulate(n, h, slot):
        m_old = m_ref[h]
        m_new = jnp.maximum(m_old, cm_ref[slot])
        p = jnp.exp2(s_ref[slot] - m_new).astype(BF16)
        pv = _dot(vt_ref[n, h], p)
        m_ref[h] = m_new
        acc_ref[h] = jnp.exp2(m_old - m_new) * acc_ref[h] + pv

    def chunk(c, carry):
        units = [(c * ATTN_UNROLL + u, h) for u in range(ATTN_UNROLL) for h in range(hp)]
        for j in range(ATTN_LOOKAHEAD):
            scores(*units[j], j % ATTN_SLOTS)
        for j, (n, h) in enumerate(units):
            if j + ATTN_LOOKAHEAD < len(units):
                scores(*units[j + ATTN_LOOKAHEAD], (j + ATTN_LOOKAHEAD) % ATTN_SLOTS)
            accumulate(n, h, j % ATTN_SLOTS)
        return carry

    lax.fori_loop(0, (qb + ATTN_UNROLL) // ATTN_UNROLL, chunk, 0)
    outs = [acc_ref[h, 0:HEAD_DIM, :] / acc_ref[h, HEAD_DIM:HEAD_DIM + 1, :] for h in range(hp)]
    o_ref[0] = jnp.concatenate(outs, axis=0).T.astype(BF16)


def _slope_rows():
    slopes = 2.0 ** (-8.0 * jnp.arange(1, N_HEADS + 1, dtype=F32) / N_HEADS) * LOG2E
    s1 = slopes.astype(BF16)
    r1 = slopes - s1.astype(F32)
    s2 = r1.astype(BF16)
    s3 = (r1 - s2.astype(F32)).astype(BF16)
    parts = jnp.stack([s1, s2, s3, s1, s2, s3] + [jnp.zeros_like(s1)] * (AUG_POS - 6), axis=1)
    return jnp.broadcast_to(parts[:, :, None], (N_HEADS, AUG_POS, MOBA_BLOCK))


def _attention(q3, kaug3, v3, gt, tm):
    b, s, aw = q3.shape
    blk = MOBA_BLOCK
    hp = HEADS_PER_GROUP
    gw = hp * HEAD_DIM
    n_groups = aw // gw
    nb = s // blk
    per_tile = tm // blk
    tiles_per_seq = s // tm
    return pl.pallas_call(
        _attn_kernel,
        grid=(b, n_groups, nb),
        in_specs=[
            pl.BlockSpec((1, blk, gw), lambda i, g, j: (i, j, g)),
            pl.BlockSpec((1, s, hp * K_AUG), lambda i, g, j: (i, 0, g)),
            pl.BlockSpec((1, s, gw), lambda i, g, j: (i, 0, g)),
            pl.BlockSpec((1, hp * GATE_SLOTS, blk),
                         lambda i, g, j: (i * tiles_per_seq + j // per_tile, g, j % per_tile)),
            pl.BlockSpec((hp, AUG_POS, blk), lambda i, g, j: (g, 0, 0)),
        ],
        out_specs=pl.BlockSpec((1, blk, gw), lambda i, g, j: (i, j, g)),
        scratch_shapes=[
            pltpu.VMEM((nb, hp, V_ROWS, blk), BF16),
            pltpu.VMEM((2, blk, blk), F32),
            pltpu.VMEM((hp, K_AUG, blk), BF16),
            pltpu.VMEM((ATTN_SLOTS, blk, blk), F32),
            pltpu.VMEM((ATTN_SLOTS, 1, blk), F32),
            pltpu.VMEM((hp, 1, blk), F32),
            pltpu.VMEM((hp, V_ROWS, blk), F32),
        ],
        out_shape=jax.ShapeDtypeStruct((b, s, aw), BF16),
        compiler_params=_cparams(("arbitrary", "arbitrary", "arbitrary")),
        name="attn",
    )(q3, kaug3, v3, gt, _slope_rows())


def _pack_bf16_pairs(x):
    n = x.shape[1] // 2
    r = x.astype(BF16).astype(F32)
    lo = lax.shift_right_logical(pltpu.bitcast(r[:, :n], U32), jnp.uint32(16))
    hi = pltpu.bitcast(r[:, n:], U32) & jnp.uint32(0xFFFF0000)
    return hi | lo


def _store_rows(ref, x):
    pk = _pack_bf16_pairs(x)
    m, n = pk.shape
    c = n // 128
    for j in range(c):
        ref[pl.ds(j, m, stride=c), :] = pk[:, j * 128:(j + 1) * 128]


def _load_rows(ref, m):
    c = ref.shape[0] // m
    ws = [ref[pl.ds(j, m, stride=c), :] for j in range(c)]
    lo = [pltpu.bitcast(lax.shift_left(w, jnp.uint32(16)), F32) for w in ws]
    hi = [pltpu.bitcast(w & jnp.uint32(0xFFFF0000), F32) for w in ws]
    return jnp.concatenate(lo + hi, axis=1)


def _mixout_kernel(x_ref, c_ref, a_ref, gate_ref, wco_ref, wao_ref, wout_ref, g2_ref,
                   wr_ref, br_ref,
                   h1_ref, u2_ref, topi_ref, topg_ref):
    d = x_ref.shape[1]
    y_c = _dot(c_ref[...], wco_ref[...])
    y_a = _dot(a_ref[...], wao_ref[...])
    gate = gate_ref[...].astype(F32)
    mixed = (gate[:, :d] * y_c + gate[:, d:] * y_a).astype(BF16)
    h1 = x_ref[...] + _dot(mixed, wout_ref[...])
    h1_ref[...] = h1
    ms = jnp.mean(h1 * h1, axis=-1, keepdims=True)
    u2 = h1 * lax.rsqrt(ms + EPS) * g2_ref[...]
    _store_rows(u2_ref, u2)
    u_hi, u_lo = _split_bf16(u2)
    wr = wr_ref[...]
    both = _dot_nt(wr, u_hi)
    logits = both[:N_EXPERTS] + both[N_EXPERTS:] + _dot_nt(wr[:N_EXPERTS], u_lo) + br_ref[...]
    eid = lax.broadcasted_iota(I32, logits.shape, 0)
    vals = []
    for k in range(TOP_K):
        mx = jnp.max(logits, axis=0, keepdims=True)
        first = jnp.min(jnp.where(logits == mx, eid, N_EXPERTS), axis=0, keepdims=True)
        topi_ref[k:k + 1, :] = first
        vals.append(mx)
        logits = jnp.where(eid == first, NEG_INF, logits)
    es = [jnp.exp(v - vals[0]) for v in vals]
    den = es[0] + es[1] + es[2] + es[3]
    for k in range(TOP_K):
        topg_ref[k:k + 1, :] = es[k] / den


def _mixout(x2, c2, a2, gates, w_conv_out, w_attn_out, w_out, norm2_g, w_router, b_router, tm):
    t, d = x2.shape
    wr_t = w_router.T
    wrh = wr_t.astype(BF16)
    wrl = (wr_t - wrh.astype(F32)).astype(BF16)
    args = (x2, c2, a2, gates, w_conv_out.astype(BF16), w_attn_out.astype(BF16), w_out.astype(BF16),
            norm2_g.reshape(1, d), jnp.concatenate([wrh, wrl], axis=0), b_router.reshape(N_EXPERTS, 1))
    full = lambda a: pl.BlockSpec(a.shape, lambda i: (0,) * a.ndim)
    row = lambda a: pl.BlockSpec((tm, a.shape[1]), lambda i: (i, 0))
    return pl.pallas_call(
        _mixout_kernel,
        grid=(t // tm,),
        in_specs=[row(a) for a in args[:4]] + [full(a) for a in args[4:]],
        out_specs=[pl.BlockSpec((tm, d), lambda i: (i, 0)), pl.BlockSpec((tm * ROW_CHUNKS, 128), lambda i: (i, 0)),
                   pl.BlockSpec((TOP_K, tm), lambda i: (0, i)), pl.BlockSpec((TOP_K, tm), lambda i: (0, i))],
        out_shape=[jax.ShapeDtypeStruct((t, d), F32), jax.ShapeDtypeStruct((t * ROW_CHUNKS, 128), U32),
                   jax.ShapeDtypeStruct((TOP_K, t), I32), jax.ShapeDtypeStruct((TOP_K, t), F32)],
        compiler_params=_cparams(("arbitrary",)),
        name="mixout",
    )(*args)


def _rank_kernel(topi_ref, tri_ref, rank_ref, cnt_ref, run_ref):
    i = pl.program_id(0)
    tr = topi_ref.shape[1]

    @pl.when(i == 0)
    def _():
        run_ref[...] = jnp.zeros_like(run_ref)

    eid = lax.broadcasted_iota(I32, (N_EXPERTS, tr), 0)
    onehots = [(eid == topi_ref[k:k + 1, :]).astype(F32) for k in range(TOP_K)]
    tot = onehots[0] + onehots[1] + onehots[2] + onehots[3]
    before = _dot(tot.astype(BF16), tri_ref[...])
    base = run_ref[...] + before
    for k in range(TOP_K):
        rank_ref[k:k + 1, :] = jnp.sum(onehots[k] * base, axis=0, keepdims=True).astype(I32)
    run_ref[...] = run_ref[...] + jnp.sum(tot, axis=1, keepdims=True)
    cnt_ref[...] = run_ref[...]


def _rank(topi, tr):
    t = topi.shape[1]
    tri = (jnp.arange(tr)[:, None] < jnp.arange(tr)[None, :]).astype(BF16)
    return pl.pallas_call(
        _rank_kernel,
        grid=(t // tr,),
        in_specs=[pl.BlockSpec((TOP_K, tr), lambda i: (0, i)), pl.BlockSpec((tr, tr), lambda i: (0, 0))],
        out_specs=[pl.BlockSpec((TOP_K, tr), lambda i: (0, i)), pl.BlockSpec((N_EXPERTS, 1), lambda i: (0, 0))],
        out_shape=[jax.ShapeDtypeStruct((TOP_K, t), I32), jax.ShapeDtypeStruct((N_EXPERTS, 1), F32)],
        scratch_shapes=[pltpu.VMEM((N_EXPERTS, 1), F32)],
        compiler_params=_cparams(("arbitrary",)),
        name="rank",
    )(topi, tri)


def _dest_kernel(n_blocks, topi_ref, rank_ref, cnt_ref, dest_ref, blke_ref):
    td = topi_ref.shape[1]
    cnt = cnt_ref[...]
    padded = jnp.floor((cnt + (EXPERT_ROWS - 1)) * (1.0 / EXPERT_ROWS)) * EXPERT_ROWS
    er = lax.broadcasted_iota(I32, (N_EXPERTS, N_EXPERTS), 0)
    ec = lax.broadcasted_iota(I32, (N_EXPERTS, N_EXPERTS), 1)
    pad_row = jnp.sum(jnp.where(er == ec, padded, 0.0), axis=0, keepdims=True)
    start = jnp.sum(jnp.where(ec < er, pad_row, 0.0), axis=1, keepdims=True)
    end = start + padded
    eid = lax.broadcasted_iota(I32, (N_EXPERTS, td), 0)
    for k in range(TOP_K):
        oh = eid == topi_ref[k:k + 1, :]
        s_k = jnp.sum(jnp.where(oh, start, 0.0), axis=0, keepdims=True)
        dest_ref[0, k:k + 1, :] = (s_k.astype(I32) + rank_ref[k:k + 1, :]) * ROW_CHUNKS
    bstart = (lax.broadcasted_iota(I32, (N_EXPERTS, n_blocks), 1) * EXPERT_ROWS).astype(F32)
    be = jnp.sum((end <= bstart).astype(I32), axis=0, keepdims=True)
    blke_ref[...] = jnp.minimum(be, N_EXPERTS - 1)


def _dest(topi, rank, cnt, td, n_blocks):
    t = topi.shape[1]
    return pl.pallas_call(
        functools.partial(_dest_kernel, n_blocks),
        grid=(t // td,),
        in_specs=[pl.BlockSpec((TOP_K, td), lambda i: (0, i)), pl.BlockSpec((TOP_K, td), lambda i: (0, i)),
                  pl.BlockSpec((N_EXPERTS, 1), lambda i: (0, 0))],
        out_specs=[pl.BlockSpec((1, TOP_K, td), lambda i: (i, 0, 0)),
                   pl.BlockSpec((1, n_blocks), lambda i: (0, 0))],
        out_shape=[jax.ShapeDtypeStruct((t // td, TOP_K, td), I32), jax.ShapeDtypeStruct((1, n_blocks), I32)],
        compiler_params=_cparams(("arbitrary",)),
        name="dest",
    )(topi, rank, cnt)


def _row_dma_all(n_tok, row_copy):
    def issue(it, carry):
        for u in range(DMA_UNROLL):
            for k in range(TOP_K):
                row_copy(it * DMA_UNROLL + u, k).start(priority=k % 2)
        return carry

    lax.fori_loop(0, n_tok // DMA_UNROLL, issue, 0)

    def drain(it, carry):
        for u in range(DMA_UNROLL):
            for k in range(TOP_K):
                row_copy(it * DMA_UNROLL + u, k).wait()
        return carry

    lax.fori_loop(0, n_tok // DMA_UNROLL, drain, 0)


def _load_dest(dest_hbm, dest_smem, sem_idx):
    cp = pltpu.make_async_copy(dest_hbm.at[pl.program_id(0)], dest_smem, sem_idx)
    cp.start()
    cp.wait()


def _dispatch_kernel(dest_hbm, u2_ref, xs_in, xs_hbm, dest_smem, sem_idx, sem):
    del xs_in
    td = u2_ref.shape[0] // ROW_CHUNKS
    _load_dest(dest_hbm, dest_smem, sem_idx)

    def row_copy(t, k):
        src = pl.multiple_of(t * ROW_CHUNKS, ROW_CHUNKS)
        dst = pl.multiple_of(dest_smem[k * td + t], ROW_CHUNKS)
        return pltpu.make_async_copy(u2_ref.at[pl.ds(src, ROW_CHUNKS)], xs_hbm.at[pl.ds(dst, ROW_CHUNKS)], sem)

    _row_dma_all(td, row_copy)


def _dispatch(dest, u2p, n_rows):
    n_tiles, per_tile = dest.shape
    td = per_tile // TOP_K
    xs0 = jnp.zeros((n_rows * ROW_CHUNKS, 128), U32)
    return pl.pallas_call(
        _dispatch_kernel,
        grid=(n_tiles,),
        in_specs=[pl.BlockSpec(memory_space=pl.ANY), pl.BlockSpec((td * ROW_CHUNKS, 128), lambda i: (i, 0)),
                  pl.BlockSpec(memory_space=pl.ANY)],
        out_specs=pl.BlockSpec(memory_space=pl.ANY),
        out_shape=jax.ShapeDtypeStruct(xs0.shape, U32),
        scratch_shapes=[pltpu.SMEM((per_tile,), I32), pltpu.SemaphoreType.DMA, pltpu.SemaphoreType.DMA],
        input_output_aliases={2: 0},
        compiler_params=_cparams(("arbitrary",)),
        name="dispatch",
    )(dest, u2p, xs0)


def _expert_kernel(blke_ref, xs_ref, wug_ref, bug_ref, wd_ref, bd_ref, ys_ref):
    del blke_ref
    f = wd_ref.shape[1]
    x = _load_rows(xs_ref, EXPERT_ROWS).astype(BF16)
    h = _dot(x, wug_ref[0]) + bug_ref[0]
    a = jnp.minimum(h[:, :f], SWIGLU_LIMIT)
    lin = jnp.clip(h[:, f:], -SWIGLU_LIMIT, SWIGLU_LIMIT)
    act = (a * _sigmoid(SWIGLU_ALPHA * a) * (lin + 1.0)).astype(BF16)
    y = _dot(act, wd_ref[0]) + bd_ref[0]
    _store_rows(ys_ref, y)


def _experts(blke, xs, w_up_gate, b_up_gate, w_down, b_down):
    e, d, f2 = w_up_gate.shape
    blk_rows = EXPERT_ROWS * ROW_CHUNKS
    n_blocks = xs.shape[0] // blk_rows
    return pl.pallas_call(
        _expert_kernel,
        grid_spec=pltpu.PrefetchScalarGridSpec(
            num_scalar_prefetch=1,
            grid=(n_blocks,),
            in_specs=[
                pl.BlockSpec((blk_rows, 128), lambda i, be: (i, 0)),
                pl.BlockSpec((1, d, f2), lambda i, be: (be[i], 0, 0)),
                pl.BlockSpec((1, 1, f2), lambda i, be: (be[i], 0, 0)),
                pl.BlockSpec((1, f2 // 2, d), lambda i, be: (be[i], 0, 0)),
                pl.BlockSpec((1, 1, d), lambda i, be: (be[i], 0, 0)),
            ],
            out_specs=pl.BlockSpec((blk_rows, 128), lambda i, be: (i, 0)),
        ),
        out_shape=jax.ShapeDtypeStruct(xs.shape, U32),
        compiler_params=_cparams(("arbitrary",)),
        name="experts",
    )(blke, xs, w_up_gate.astype(BF16), b_up_gate.reshape(e, 1, f2), w_down.astype(BF16),
      b_down.reshape(e, 1, d))


def _combine_kernel(dest_hbm, ys_hbm, h1_ref, topg_ref, o_ref, dest_smem, b0, b1, b2, b3, sem_idx, sem):
    tc = h1_ref.shape[0]
    bufs = (b0, b1, b2, b3)
    _load_dest(dest_hbm, dest_smem, sem_idx)

    def row_copy(t, k):
        src = pl.multiple_of(dest_smem[k * tc + t], ROW_CHUNKS)
        dst = pl.multiple_of(t * ROW_CHUNKS, ROW_CHUNKS)
        return pltpu.make_async_copy(ys_hbm.at[pl.ds(src, ROW_CHUNKS)], bufs[k].at[pl.ds(dst, ROW_CHUNKS)], sem)

    _row_dma_all(tc, row_copy)

    g = jnp.concatenate([topg_ref[...], jnp.zeros((8 - TOP_K, tc), F32)], axis=0).T
    out = h1_ref[...]
    for k in range(TOP_K):
        out = out + g[:, k:k + 1] * _load_rows(bufs[k], tc)
    o_ref[...] = out


def _combine(dest, ys, h1, topg):
    n_tiles, per_tile = dest.shape
    tc = per_tile // TOP_K
    t, d = h1.shape
    return pl.pallas_call(
        _combine_kernel,
        grid=(n_tiles,),
        in_specs=[pl.BlockSpec(memory_space=pl.ANY), pl.BlockSpec(memory_space=pl.ANY),
                  pl.BlockSpec((tc, d), lambda i: (i, 0)), pl.BlockSpec((TOP_K, tc), lambda i: (0, i))],
        out_specs=pl.BlockSpec((tc, d), lambda i: (i, 0)),
        out_shape=jax.ShapeDtypeStruct((t, d), F32),
        scratch_shapes=[pltpu.SMEM((per_tile,), I32)] + [pltpu.VMEM((tc * ROW_CHUNKS, 128), U32)] * TOP_K
                       + [pltpu.SemaphoreType.DMA, pltpu.SemaphoreType.DMA],
        compiler_params=_cparams(("arbitrary",)),
        name="combine",
    )(dest, ys, h1, topg)


def _tile(n, pref):
    return pref if n % pref == 0 else n


def _layer(h, norm1_g, w_in, b_gates, conv_dw, conv_dw_b, conv_ln_g, conv_ln_b, w_conv_out,
           q_norm_g, k_norm_g, w_attn_out, w_out, norm2_g, w_router, b_router,
           w_up_gate, b_up_gate, w_down, b_down):
    b, s, d = h.shape
    t = b * s
    assert s % MOBA_BLOCK == 0 and s // MOBA_BLOCK <= GATE_SLOTS
    tm = _tile(s, 512)
    x2 = h.reshape(t, d)
    hglu, q, k, v, gates, gt = _inproj(x2, norm1_g, w_in, b_gates, q_norm_g, k_norm_g, s, tm)
    c = hglu.shape[1]
    aw = q.shape[1]
    conv = _conv(hglu.reshape(b, s, c), conv_dw, conv_dw_b, conv_ln_g, conv_ln_b, tm)
    attn = _attention(q.reshape(b, s, aw), k.reshape(b, s, N_HEADS * K_AUG), v.reshape(b, s, aw), gt, tm)
    h1, u2p, topi, topg = _mixout(x2, conv.reshape(t, c), attn.reshape(t, aw), gates, w_conv_out, w_attn_out,
                                  w_out, norm2_g, w_router, b_router, tm)
    n_blocks = -(-(t * TOP_K) // EXPERT_ROWS) + N_EXPERTS
    rank, cnt = _rank(topi, tm)
    dest, blke = _dest(topi, rank, cnt, tm, n_blocks)
    dest = dest.reshape(t // tm, TOP_K * tm)
    xs = _dispatch(dest, u2p, n_blocks * EXPERT_ROWS)
    ys = _experts(blke.reshape(n_blocks), xs, w_up_gate, b_up_gate, w_down, b_down)
    out = _combine(dest, ys, h1, topg)
    return out.reshape(b, s, d)


def kernel(x, norm1_g, w_in, b_gates, conv_dw, conv_dw_b, conv_ln_g, conv_ln_b, w_conv_out, q_norm_g, k_norm_g,
           w_attn_out, w_out, norm2_g, w_router, b_router, w_up_gate, b_up_gate, w_down, b_down):
    h = x
    for l in range(norm1_g.shape[0]):
        h = _layer(h, norm1_g[l], w_in[l], b_gates[l], conv_dw[l], conv_dw_b[l], conv_ln_g[l], conv_ln_b[l],
                   w_conv_out[l], q_norm_g[l], k_norm_g[l], w_attn_out[l], w_out[l], norm2_g[l], w_router[l],
                   b_router[l], w_up_gate[l], b_up_gate[l], w_down[l], b_down[l])
    return h
```

```python
import functools

import jax
import jax.numpy as jnp
from jax import lax
from jax.experimental import pallas as pl
from jax.experimental.pallas import tpu as pltpu

F32 = jnp.float32
BF16 = jnp.bfloat16
I32 = jnp.int32
U32 = jnp.uint32

EPS = 1e-6
CONV_WIDTH = 31
HALO = 32
N_HEADS = 8
HEAD_DIM = 64
MOBA_BLOCK = 256
MOBA_TOPK = 3
GATE_SLOTS = 16
HEADS_PER_GROUP = 4
K_AUG = 128
AUG_POS = 16
V_ROWS = 80
MASKED = -1e30
LOG2E = 1.4426950408889634
M_INIT = -3e38
ATTN_UNROLL = 4
ATTN_LOOKAHEAD = 4
ATTN_SLOTS = 6
N_EXPERTS = 32
TOP_K = 4
SWIGLU_LIMIT = 7.0
SWIGLU_ALPHA = 1.702
EXPERT_ROWS = 512
ROW_CHUNKS = 4
DMA_UNROLL = 8
NEG_INF = float("-inf")
VMEM_LIMIT = 56 * 1024 * 1024


def _cparams(sem):
    return pltpu.CompilerParams(dimension_semantics=sem, vmem_limit_bytes=VMEM_LIMIT)


def _dot(a, b):
    return jnp.dot(a, b, preferred_element_type=F32)


def _dot_nt(a, b):
    return lax.dot_general(a, b, (((1,), (1,)), ((), ())), preferred_element_type=F32)


def _split_bf16(x):
    hi = x.astype(BF16)
    lo = (x - hi.astype(F32)).astype(BF16)
    return hi, lo


def _sigmoid(x):
    return 1.0 / (1.0 + jnp.exp(-x))


def _inproj_kernel(tiles_per_seq, x_ref, g1_ref, wglu_ref, wq_ref, wk_ref, wv_ref, wg_ref, bg_ref,
                   qg_ref, kg_ref, bd_ref, place_ref,
                   h_ref, q_ref, k_ref, v_ref, gate_ref, gt_ref, km_ref):
    i = pl.program_id(0)
    tm = x_ref.shape[0]
    c = h_ref.shape[1]
    aw = q_ref.shape[1]

    @pl.when(i == 0)
    def _():
        km_ref[...] = jnp.zeros_like(km_ref)

    x = x_ref[...]
    ms = jnp.mean(x * x, axis=-1, keepdims=True)
    u = (x * lax.rsqrt(ms + EPS) * g1_ref[...]).astype(BF16)

    glu = _dot(u, wglu_ref[...])
    h_ref[...] = (glu[:, :c] * _sigmoid(glu[:, c:])).astype(BF16)

    bd = bd_ref[...]

    def head_norm(t, g):
        ss = _dot((t * t).astype(BF16), bd)
        return t * lax.rsqrt(ss * (1.0 / HEAD_DIM) + EPS) * g

    qn = head_norm(_dot(u, wq_ref[...]), qg_ref[...]) * (HEAD_DIM ** -0.5 * LOG2E)
    kn = head_norm(_dot(u, wk_ref[...]), kg_ref[...])
    q_hi, q_lo = _split_bf16(qn)
    q_ref[...] = q_hi
    pos = lax.broadcasted_iota(I32, (tm, K_AUG), 0) + (i % tiles_per_seq) * tm
    col = lax.broadcasted_iota(I32, (tm, K_AUG), 1)
    kpos = pos & (MOBA_BLOCK - 1)
    bstart = pos - kpos
    aug = jnp.where((col >= HEAD_DIM) & (col < HEAD_DIM + 3), kpos,
                    jnp.where((col >= HEAD_DIM + 3) & (col < HEAD_DIM + 6), bstart,
                              jnp.where((col - (HEAD_DIM + AUG_POS)) * MOBA_BLOCK == bstart, 1, 0))).astype(F32)
    kp = _dot(kn.astype(BF16), place_ref[...])
    for h in range(N_HEADS):
        k_ref[:, h * K_AUG:(h + 1) * K_AUG] = (kp[:, h * K_AUG:(h + 1) * K_AUG] + aug).astype(BF16)
    v_ref[...] = _dot(u, wv_ref[...]).astype(BF16)
    gate_ref[...] = _sigmoid(_dot(u, wg_ref[...]) + bg_ref[...]).astype(BF16)

    lane_head = lax.broadcasted_iota(I32, (1, aw), 1) // HEAD_DIM
    blocks_per_tile = tm // MOBA_BLOCK
    n0 = (i % tiles_per_seq) * blocks_per_tile
    for j in range(blocks_per_tile):
        kmj = jnp.mean(kn[j * MOBA_BLOCK:(j + 1) * MOBA_BLOCK], axis=0, keepdims=True)
        for h in range(N_HEADS):
            km_ref[pl.ds(h * GATE_SLOTS + n0 + j, 1), :] = jnp.where(lane_head == h, kmj, 0.0)

    km_hi, km_lo = _split_bf16(km_ref[...])
    gt_ref[0] = _dot_nt(km_hi, q_hi) + _dot_nt(km_lo, q_hi) + _dot_nt(km_hi, q_lo)


def _inproj(x2, norm1_g, w_in, b_gates, q_norm_g, k_norm_g, seq, tm):
    t, d = x2.shape
    aw = N_HEADS * HEAD_DIM
    c = (w_in.shape[1] - 3 * aw - 2 * d) // 2
    c1, c2, c3, c4 = 2 * c, 2 * c + aw, 2 * c + 2 * aw, 2 * c + 3 * aw
    wb = w_in.astype(BF16)
    hd = jnp.arange(aw, dtype=I32) // HEAD_DIM
    bd = (hd[:, None] == hd[None, :]).astype(BF16)
    lane = jnp.arange(aw, dtype=I32)
    place = ((lane // HEAD_DIM * K_AUG + lane % HEAD_DIM)[:, None]
             == jnp.arange(N_HEADS * K_AUG, dtype=I32)[None, :]).astype(BF16)
    n_tiles = t // tm
    full = lambda a: pl.BlockSpec(a.shape, lambda i: (0,) * a.ndim)
    row = lambda n: pl.BlockSpec((tm, n), lambda i: (i, 0))
    args = (x2, norm1_g.reshape(1, d), wb[:, :c1], wb[:, c1:c2], wb[:, c2:c3], wb[:, c3:c4], wb[:, c4:],
            b_gates.reshape(1, 2 * d), jnp.tile(q_norm_g, N_HEADS).reshape(1, aw),
            jnp.tile(k_norm_g, N_HEADS).reshape(1, aw), bd, place)
    return pl.pallas_call(
        functools.partial(_inproj_kernel, seq // tm),
        grid=(n_tiles,),
        in_specs=[row(d)] + [full(a) for a in args[1:]],
        out_specs=[row(c), row(aw), row(N_HEADS * K_AUG), row(aw), row(2 * d),
                   pl.BlockSpec((1, N_HEADS * GATE_SLOTS, tm), lambda i: (i, 0, 0))],
        out_shape=[jax.ShapeDtypeStruct((t, c), BF16), jax.ShapeDtypeStruct((t, aw), BF16),
                   jax.ShapeDtypeStruct((t, N_HEADS * K_AUG), BF16), jax.ShapeDtypeStruct((t, aw), BF16),
                   jax.ShapeDtypeStruct((t, 2 * d), BF16),
                   jax.ShapeDtypeStruct((n_tiles, N_HEADS * GATE_SLOTS, tm), F32)],
        scratch_shapes=[pltpu.VMEM((N_HEADS * GATE_SLOTS, aw), F32)],
        compiler_params=_cparams(("arbitrary",)),
        name="inproj",
    )(*args)


def _conv_kernel(h_ref, dw_ref, dwb_ref, lng_ref, lnb_ref, o_ref, win_ref, sh_ref):
    s = pl.program_id(1)
    ts = h_ref.shape[1]

    @pl.when(s == 0)
    def _():
        win_ref[0:HALO, :] = jnp.zeros((HALO, win_ref.shape[1]), F32)

    @pl.when(s != 0)
    def _():
        win_ref[0:HALO, :] = win_ref[ts:ts + HALO, :]

    win_ref[HALO:HALO + ts, :] = h_ref[0].astype(F32)
    span = ts + HALO - 8
    for r in range(1, 8):
        sh_ref[r - 1] = win_ref[r:r + span, :]
    acc = jnp.zeros((ts, win_ref.shape[1]), F32) + dwb_ref[...]
    for w in range(CONV_WIDTH):
        off = HALO - (CONV_WIDTH - 1) + w
        r = off % 8
        tap = win_ref[off:off + ts, :] if r == 0 else sh_ref[r - 1, off - r:off - r + ts, :]
        acc = acc + dw_ref[w:w + 1, :] * tap
    mu = jnp.mean(acc, axis=-1, keepdims=True)
    cen = acc - mu
    var = jnp.mean(cen * cen, axis=-1, keepdims=True)
    y = cen * lax.rsqrt(var + EPS) * lng_ref[...] + lnb_ref[...]
    o_ref[0] = (y * _sigmoid(y)).astype(BF16)


def _conv(h3, dw, dwb, lng, lnb, ts):
    b, s, c = h3.shape
    vec = lambda a: pl.BlockSpec(a.shape, lambda i, j: (0, 0))
    args = (h3, dw, dwb.reshape(1, c), lng.reshape(1, c), lnb.reshape(1, c))
    return pl.pallas_call(
        _conv_kernel,
        grid=(b, s // ts),
        in_specs=[pl.BlockSpec((1, ts, c), lambda i, j: (i, j, 0))] + [vec(a) for a in args[1:]],
        out_specs=pl.BlockSpec((1, ts, c), lambda i, j: (i, j, 0)),
        out_shape=jax.ShapeDtypeStruct((b, s, c), BF16),
        scratch_shapes=[pltpu.VMEM((ts + HALO, c), F32), pltpu.VMEM((7, ts + HALO - 8, c), F32)],
        compiler_params=_cparams(("arbitrary", "arbitrary")),
        name="conv",
    )(*args)


def _attn_kernel(q_ref, k_ref, v_ref, gt_ref, srow_ref, o_ref,
                 vt_ref, caus_ref, w_ref, s_ref, cm_ref, m_ref, acc_ref):
    qb = pl.program_id(2)
    blk = MOBA_BLOCK
    hp = HEADS_PER_GROUP
    n_blocks = v_ref.shape[1] // blk

    @pl.when(qb == 0)
    def _():
        ones_rows = (lax.broadcasted_iota(I32, (V_ROWS - HEAD_DIM, blk), 0) == 0).astype(BF16)
        for n in range(n_blocks):
            vt = v_ref[0, n * blk:(n + 1) * blk, :].astype(F32).T
            for h in range(hp):
                vt_ref[n, h, 0:HEAD_DIM, :] = vt[h * HEAD_DIM:(h + 1) * HEAD_DIM].astype(BF16)
                vt_ref[n, h, HEAD_DIM:V_ROWS, :] = ones_rows
        kpos = lax.broadcasted_iota(I32, (blk, blk), 0)
        qpos = lax.broadcasted_iota(I32, (blk, blk), 1)
        caus_ref[0] = jnp.zeros((blk, blk), F32)
        caus_ref[1] = jnp.where(kpos <= qpos, 0.0, NEG_INF)
        for h in range(hp):
            w_ref[h, HEAD_DIM:HEAD_DIM + AUG_POS, :] = srow_ref[h]
            w_ref[h, HEAD_DIM + AUG_POS + GATE_SLOTS:K_AUG, :] = jnp.zeros(
                (K_AUG - HEAD_DIM - AUG_POS - GATE_SLOTS, blk), BF16)

    qt = q_ref[0].astype(F32).T
    slot = lax.broadcasted_iota(I32, (GATE_SLOTS, blk), 0)
    for h in range(hp):
        g = gt_ref[0, h * GATE_SLOTS:(h + 1) * GATE_SLOTS, :]
        g = jnp.where(slot < qb, g, NEG_INF)
        sel = slot == qb
        for _ in range(MOBA_TOPK):
            mx = jnp.max(g, axis=0, keepdims=True)
            first = jnp.min(jnp.where(g == mx, slot, GATE_SLOTS), axis=0, keepdims=True)
            pick = (slot == first) & (mx > NEG_INF)
            sel = sel | pick
            g = jnp.where(pick, NEG_INF, g)
        w_ref[h, 0:HEAD_DIM, :] = qt[h * HEAD_DIM:(h + 1) * HEAD_DIM].astype(BF16)
        w_ref[h, HEAD_DIM + AUG_POS:HEAD_DIM + AUG_POS + GATE_SLOTS, :] = jnp.where(sel, 0.0, MASKED).astype(BF16)

    m_ref[...] = jnp.full(m_ref.shape, M_INIT, F32)
    acc_ref[...] = jnp.zeros(acc_ref.shape, F32)

    def scores(n, h, slot):
        start = pl.multiple_of(n * blk, blk)
        t = _dot(k_ref[0, pl.ds(start, blk), h * K_AUG:(h + 1) * K_AUG], w_ref[h])
        t = t + caus_ref[jnp.where(n == qb, 1, 0)]
        s_ref[slot] = t
        cm_ref[slot] = jnp.max(t, axis=0, keepdims=True)

    def accumulate(n, h, slot):
        m_old = m_ref[h]
        m_new = jnp.maximum(m_old, cm_ref[slot])
        p = jnp.exp2(s_ref[slot] - m_new).astype(BF16)
        pv = _dot(vt_ref[n, h], p)
        m_ref[h] = m_new
        acc_ref[h] = jnp.exp2(m_old - m_new) * acc_ref[h] + pv

    def chunk(c, carry):
        units = [(c * ATTN_UNROLL + u, h) for u in range(ATTN_UNROLL) for h in range(hp)]
        for j in range(ATTN_LOOKAHEAD):
            scores(*units[j], j % ATTN_SLOTS)
        for j, (n, h) in enumerate(units):
            if j + ATTN_LOOKAHEAD < len(units):
                scores(*units[j + ATTN_LOOKAHEAD], (j + ATTN_LOOKAHEAD) % ATTN_SLOTS)
            accumulate(n, h, j % ATTN_SLOTS)
        return carry

    lax.fori_loop(0, (qb + ATTN_UNROLL) // ATTN_UNROLL, chunk, 0)
    outs = [acc_ref[h, 0:HEAD_DIM, :] / acc_ref[h, HEAD_DIM:HEAD_DIM + 1, :] for h in range(hp)]
    o_ref[0] = jnp.concatenate(outs, axis=0).T.astype(BF16)


def _slope_rows():
    slopes = 2.0 ** (-8.0 * jnp.arange(1, N_HEADS + 1, dtype=F32) / N_HEADS) * LOG2E
    s1 = slopes.astype(BF16)
    r1 = slopes - s1.astype(F32)
    s2 = r1.astype(BF16)
    s3 = (r1 - s2.astype(F32)).astype(BF16)
    parts = jnp.stack([s1, s2, s3, s1, s2, s3] + [jnp.zeros_like(s1)] * (AUG_POS - 6), axis=1)
    return jnp.broadcast_to(parts[:, :, None], (N_HEADS, AUG_POS, MOBA_BLOCK))


def _attention(q3, kaug3, v3, gt, tm):
    b, s, aw = q3.shape
    blk = MOBA_BLOCK
    hp = HEADS_PER_GROUP
    gw = hp * HEAD_DIM
    n_groups = aw // gw
    nb = s // blk
    per_tile = tm // blk
    tiles_per_seq = s // tm
    return pl.pallas_call(
        _attn_kernel,
        grid=(b, n_groups, nb),
        in_specs=[
            pl.BlockSpec((1, blk, gw), lambda i, g, j: (i, j, g)),
            pl.BlockSpec((1, s, hp * K_AUG), lambda i, g, j: (i, 0, g)),
            pl.BlockSpec((1, s, gw), lambda i, g, j: (i, 0, g)),
            pl.BlockSpec((1, hp * GATE_SLOTS, blk),
                         lambda i, g, j: (i * tiles_per_seq + j // per_tile, g, j % per_tile)),
            pl.BlockSpec((hp, AUG_POS, blk), lambda i, g, j: (g, 0, 0)),
        ],
        out_specs=pl.BlockSpec((1, blk, gw), lambda i, g, j: (i, j, g)),
        scratch_shapes=[
            pltpu.VMEM((nb, hp, V_ROWS, blk), BF16),
            pltpu.VMEM((2, blk, blk), F32),
            pltpu.VMEM((hp, K_AUG, blk), BF16),
            pltpu.VMEM((ATTN_SLOTS, blk, blk), F32),
            pltpu.VMEM((ATTN_SLOTS, 1, blk), F32),
            pltpu.VMEM((hp, 1, blk), F32),
            pltpu.VMEM((hp, V_ROWS, blk), F32),
        ],
        out_shape=jax.ShapeDtypeStruct((b, s, aw), BF16),
        compiler_params=_cparams(("arbitrary", "arbitrary", "arbitrary")),
        name="attn",
    )(q3, kaug3, v3, gt, _slope_rows())


def _pack_bf16_pairs(x):
    n = x.shape[1] // 2
    r = x.astype(BF16).astype(F32)
    lo = lax.shift_right_logical(pltpu.bitcast(r[:, :n], U32), jnp.uint32(16))
    hi = pltpu.bitcast(r[:, n:], U32) & jnp.uint32(0xFFFF0000)
    return hi | lo


def _store_rows(ref, x):
    pk = _pack_bf16_pairs(x)
    m, n = pk.shape
    c = n // 128
    for j in range(c):
        ref[pl.ds(j, m, stride=c), :] = pk[:, j * 128:(j + 1) * 128]


def _load_rows(ref, m):
    c = ref.shape[0] // m
    ws = [ref[pl.ds(j, m, stride=c), :] for j in range(c)]
    lo = [pltpu.bitcast(lax.shift_left(w, jnp.uint32(16)), F32) for w in ws]
    hi = [pltpu.bitcast(w & jnp.uint32(0xFFFF0000), F32) for w in ws]
    return jnp.concatenate(lo + hi, axis=1)


def _mixout_kernel(x_ref, c_ref, a_ref, gate_ref, wco_ref, wao_ref, wout_ref, g2_ref,
                   wr_ref, br_ref,
                   h1_ref, u2_ref, topi_ref, topg_ref):
    d = x_ref.shape[1]
    y_c = _dot(c_ref[...], wco_ref[...])
    y_a = _dot(a_ref[...], wao_ref[...])
    gate = gate_ref[...].astype(F32)
    mixed = (gate[:, :d] * y_c + gate[:, d:] * y_a).astype(BF16)
    h1 = x_ref[...] + _dot(mixed, wout_ref[...])
    h1_ref[...] = h1
    ms = jnp.mean(h1 * h1, axis=-1, keepdims=True)
    u2 = h1 * lax.rsqrt(ms + EPS) * g2_ref[...]
    _store_rows(u2_ref, u2)
    u_hi, u_lo = _split_bf16(u2)
    wr = wr_ref[...]
    both = _dot_nt(wr, u_hi)
    logits = both[:N_EXPERTS] + both[N_EXPERTS:] + _dot_nt(wr[:N_EXPERTS], u_lo) + br_ref[...]
    eid = lax.broadcasted_iota(I32, logits.shape, 0)
    vals = []
    for k in range(TOP_K):
        mx = jnp.max(logits, axis=0, keepdims=True)
        first = jnp.min(jnp.where(logits == mx, eid, N_EXPERTS), axis=0, keepdims=True)
        topi_ref[k:k + 1, :] = first
        vals.append(mx)
        logits = jnp.where(eid == first, NEG_INF, logits)
    es = [jnp.exp(v - vals[0]) for v in vals]
    den = es[0] + es[1] + es[2] + es[3]
    for k in range(TOP_K):
        topg_ref[k:k + 1, :] = es[k] / den


def _mixout(x2, c2, a2, gates, w_conv_out, w_attn_out, w_out, norm2_g, w_router, b_router, tm):
    t, d = x2.shape
    wr_t = w_router.T
    wrh = wr_t.astype(BF16)
    wrl = (wr_t - wrh.astype(F32)).astype(BF16)
    args = (x2, c2, a2, gates, w_conv_out.astype(BF16), w_attn_out.astype(BF16), w_out.astype(BF16),
            norm2_g.reshape(1, d), jnp.concatenate([wrh, wrl], axis=0), b_router.reshape(N_EXPERTS, 1))
    full = lambda a: pl.BlockSpec(a.shape, lambda i: (0,) * a.ndim)
    row = lambda a: pl.BlockSpec((tm, a.shape[1]), lambda i: (i, 0))
    return pl.pallas_call(
        _mixout_kernel,
        grid=(t // tm,),
        in_specs=[row(a) for a in args[:4]] + [full(a) for a in args[4:]],
        out_specs=[pl.BlockSpec((tm, d), lambda i: (i, 0)), pl.BlockSpec((tm * ROW_CHUNKS, 128), lambda i: (i, 0)),
                   pl.BlockSpec((TOP_K, tm), lambda i: (0, i)), pl.BlockSpec((TOP_K, tm), lambda i: (0, i))],
        out_shape=[jax.ShapeDtypeStruct((t, d), F32), jax.ShapeDtypeStruct((t * ROW_CHUNKS, 128), U32),
                   jax.ShapeDtypeStruct((TOP_K, t), I32), jax.ShapeDtypeStruct((TOP_K, t), F32)],
        compiler_params=_cparams(("arbitrary",)),
        name="mixout",
    )(*args)


def _rank_kernel(topi_ref, tri_ref, rank_ref, cnt_ref, run_ref):
    i = pl.program_id(0)
    tr = topi_ref.shape[1]

    @pl.when(i == 0)
    def _():
        run_ref[...] = jnp.zeros_like(run_ref)

    eid = lax.broadcasted_iota(I32, (N_EXPERTS, tr), 0)
    onehots = [(eid == topi_ref[k:k + 1, :]).astype(F32) for k in range(TOP_K)]
    tot = onehots[0] + onehots[1] + onehots[2] + onehots[3]
    before = _dot(tot.astype(BF16), tri_ref[...])
    base = run_ref[...] + before
    for k in range(TOP_K):
        rank_ref[k:k + 1, :] = jnp.sum(onehots[k] * base, axis=0, keepdims=True).astype(I32)
    run_ref[...] = run_ref[...] + jnp.sum(tot, axis=1, keepdims=True)
    cnt_ref[...] = run_ref[...]


def _rank(topi, tr):
    t = topi.shape[1]
    tri = (jnp.arange(tr)[:, None] < jnp.arange(tr)[None, :]).astype(BF16)
    return pl.pallas_call(
        _rank_kernel,
        grid=(t // tr,),
        in_specs=[pl.BlockSpec((TOP_K, tr), lambda i: (0, i)), pl.BlockSpec((tr, tr), lambda i: (0, 0))],
        out_specs=[pl.BlockSpec((TOP_K, tr), lambda i: (0, i)), pl.BlockSpec((N_EXPERTS, 1), lambda i: (0, 0))],
        out_shape=[jax.ShapeDtypeStruct((TOP_K, t), I32), jax.ShapeDtypeStruct((N_EXPERTS, 1), F32)],
        scratch_shapes=[pltpu.VMEM((N_EXPERTS, 1), F32)],
        compiler_params=_cparams(("arbitrary",)),
        name="rank",
    )(topi, tri)


def _dest_kernel(n_blocks, topi_ref, rank_ref, cnt_ref, dest_ref, blke_ref, meta_ref):
    td = topi_ref.shape[1]
    cnt = cnt_ref[...]
    padded = jnp.floor((cnt + (EXPERT_ROWS - 1)) * (1.0 / EXPERT_ROWS)) * EXPERT_ROWS
    er = lax.broadcasted_iota(I32, (N_EXPERTS, N_EXPERTS), 0)
    ec = lax.broadcasted_iota(I32, (N_EXPERTS, N_EXPERTS), 1)
    pad_row = jnp.sum(jnp.where(er == ec, padded, 0.0), axis=0, keepdims=True)
    start = jnp.sum(jnp.where(ec < er, pad_row, 0.0), axis=1, keepdims=True)
    end = start + padded
    eid = lax.broadcasted_iota(I32, (N_EXPERTS, td), 0)
    for k in range(TOP_K):
        oh = eid == topi_ref[k:k + 1, :]
        s_k = jnp.sum(jnp.where(oh, start, 0.0), axis=0, keepdims=True)
        dest_ref[0, k:k + 1, :] = (s_k.astype(I32) + rank_ref[k:k + 1, :]) * ROW_CHUNKS
    bstart = (lax.broadcasted_iota(I32, (N_EXPERTS, n_blocks), 1) * EXPERT_ROWS).astype(F32)
    be = jnp.sum((end <= bstart).astype(I32), axis=0, keepdims=True)
    blke_ref[...] = jnp.minimum(be, N_EXPERTS - 1)
    er2 = lax.broadcasted_iota(I32, (N_EXPERTS, 128), 0)
    lane = lax.broadcasted_iota(I32, (N_EXPERTS, 128), 1)
    last = jnp.maximum(end * (1.0 / EXPERT_ROWS) - 1.0, 0.0)
    used = jnp.where(er2 == N_EXPERTS - 1, end * (1.0 / EXPERT_ROWS), 0.0)
    meta = jnp.where(lane == er2, last, 0.0) + jnp.where(lane == N_EXPERTS, used, 0.0)
    meta_ref[...] = jnp.sum(meta, axis=0, keepdims=True).astype(I32)


def _dest(topi, rank, cnt, td, n_blocks):
    t = topi.shape[1]
    return pl.pallas_call(
        functools.partial(_dest_kernel, n_blocks),
        grid=(t // td,),
        in_specs=[pl.BlockSpec((TOP_K, td), lambda i: (0, i)), pl.BlockSpec((TOP_K, td), lambda i: (0, i)),
                  pl.BlockSpec((N_EXPERTS, 1), lambda i: (0, 0))],
        out_specs=[pl.BlockSpec((1, TOP_K, td), lambda i: (i, 0, 0)),
                   pl.BlockSpec((1, n_blocks), lambda i: (0, 0)), pl.BlockSpec((1, 128), lambda i: (0, 0))],
        out_shape=[jax.ShapeDtypeStruct((t // td, TOP_K, td), I32), jax.ShapeDtypeStruct((1, n_blocks), I32),
                   jax.ShapeDtypeStruct((1, 128), I32)],
        compiler_params=_cparams(("arbitrary",)),
        name="dest",
    )(topi, rank, cnt)


def _row_dma_all(n_tok, row_copy):
    def issue(it, carry):
        for u in range(DMA_UNROLL):
            for k in range(TOP_K):
                row_copy(it * DMA_UNROLL + u, k).start(priority=k % 2)
        return carry

    lax.fori_loop(0, n_tok // DMA_UNROLL, issue, 0)

    def drain(it, carry):
        for u in range(DMA_UNROLL):
            for k in range(TOP_K):
                row_copy(it * DMA_UNROLL + u, k).wait()
        return carry

    lax.fori_loop(0, n_tok // DMA_UNROLL, drain, 0)


def _load_dest(dest_hbm, dest_smem, sem_idx):
    cp = pltpu.make_async_copy(dest_hbm.at[pl.program_id(0)], dest_smem, sem_idx)
    cp.start()
    cp.wait()


def _dispatch_kernel(dest_hbm, u2_ref, xs_in, xs_hbm, dest_smem, sem_idx, sem):
    del xs_in
    td = u2_ref.shape[0] // ROW_CHUNKS
    _load_dest(dest_hbm, dest_smem, sem_idx)

    def row_copy(t, k):
        src = pl.multiple_of(t * ROW_CHUNKS, ROW_CHUNKS)
        dst = pl.multiple_of(dest_smem[k * td + t], ROW_CHUNKS)
        return pltpu.make_async_copy(u2_ref.at[pl.ds(src, ROW_CHUNKS)], xs_hbm.at[pl.ds(dst, ROW_CHUNKS)], sem)

    _row_dma_all(td, row_copy)


def _zfill_kernel(meta_ref, o_ref):
    del meta_ref
    o_ref[...] = jnp.zeros(o_ref.shape, U32)


def _dispatch(dest, u2p, meta, n_rows):
    n_tiles, per_tile = dest.shape
    td = per_tile // TOP_K
    blk_rows = EXPERT_ROWS * ROW_CHUNKS
    xs0 = pl.pallas_call(
        _zfill_kernel,
        grid_spec=pltpu.PrefetchScalarGridSpec(
            num_scalar_prefetch=1, grid=(N_EXPERTS,), in_specs=[],
            out_specs=pl.BlockSpec((blk_rows, 128), lambda e, mt: (mt[e], 0))),
        out_shape=jax.ShapeDtypeStruct((n_rows * ROW_CHUNKS, 128), U32),
        compiler_params=_cparams(("arbitrary",)),
        name="zfill",
    )(meta)
    return pl.pallas_call(
        _dispatch_kernel,
        grid=(n_tiles,),
        in_specs=[pl.BlockSpec(memory_space=pl.ANY), pl.BlockSpec((td * ROW_CHUNKS, 128), lambda i: (i, 0)),
                  pl.BlockSpec(memory_space=pl.ANY)],
        out_specs=pl.BlockSpec(memory_space=pl.ANY),
        out_shape=jax.ShapeDtypeStruct(xs0.shape, U32),
        scratch_shapes=[pltpu.SMEM((per_tile,), I32), pltpu.SemaphoreType.DMA, pltpu.SemaphoreType.DMA],
        input_output_aliases={2: 0},
        compiler_params=_cparams(("arbitrary",)),
        name="dispatch",
    )(dest, u2p, xs0)


def _used_block(i, meta_ref):
    return jnp.minimum(i, meta_ref[N_EXPERTS] - 1)


def _expert_kernel(blke_ref, meta_ref, xs_ref, wug_ref, bug_ref, wd_ref, bd_ref, ys_ref, wug_bf, wd_bf):
    i = pl.program_id(0)
    f = wd_ref.shape[1]
    cur = blke_ref[_used_block(i, meta_ref)]
    prev = blke_ref[_used_block(jnp.maximum(i - 1, 0), meta_ref)]

    @pl.when((i == 0) | (cur != prev))
    def _():
        wug_bf[...] = wug_ref[0].astype(BF16)
        wd_bf[...] = wd_ref[0].astype(BF16)

    @pl.when(i < meta_ref[N_EXPERTS])
    def _():
        x = _load_rows(xs_ref, EXPERT_ROWS).astype(BF16)
        h = _dot(x, wug_bf[...]) + bug_ref[0]
        a = jnp.minimum(h[:, :f], SWIGLU_LIMIT)
        lin = jnp.clip(h[:, f:], -SWIGLU_LIMIT, SWIGLU_LIMIT)
        act = (a * _sigmoid(SWIGLU_ALPHA * a) * (lin + 1.0)).astype(BF16)
        y = _dot(act, wd_bf[...]) + bd_ref[0]
        _store_rows(ys_ref, y)


def _experts(blke, meta, xs, w_up_gate, b_up_gate, w_down, b_down):
    e, d, f2 = w_up_gate.shape
    blk_rows = EXPERT_ROWS * ROW_CHUNKS
    n_blocks = xs.shape[0] // blk_rows
    row_blk = lambda i, be, mt: (_used_block(i, mt), 0)
    expert = lambda i, be, mt: (be[_used_block(i, mt)], 0, 0)
    return pl.pallas_call(
        _expert_kernel,
        grid_spec=pltpu.PrefetchScalarGridSpec(
            num_scalar_prefetch=2,
            grid=(n_blocks,),
            in_specs=[
                pl.BlockSpec((blk_rows, 128), row_blk),
                pl.BlockSpec((1, d, f2), expert),
                pl.BlockSpec((1, 1, f2), expert),
                pl.BlockSpec((1, f2 // 2, d), expert),
                pl.BlockSpec((1, 1, d), expert),
            ],
            out_specs=pl.BlockSpec((blk_rows, 128), row_blk),
            scratch_shapes=[pltpu.VMEM((d, f2), BF16), pltpu.VMEM((f2 // 2, d), BF16)],
        ),
        out_shape=jax.ShapeDtypeStruct(xs.shape, U32),
        compiler_params=_cparams(("arbitrary",)),
        name="experts",
    )(blke, meta, xs, w_up_gate, b_up_gate.reshape(e, 1, f2), w_down, b_down.reshape(e, 1, d))


def _combine_kernel(dest_hbm, ys_hbm, h1_ref, topg_ref, o_ref, dest_smem, b0, b1, b2, b3, sem_idx, sem):
    tc = h1_ref.shape[0]
    bufs = (b0, b1, b2, b3)
    _load_dest(dest_hbm, dest_smem, sem_idx)

    def row_copy(t, k):
        src = pl.multiple_of(dest_smem[k * tc + t], ROW_CHUNKS)
        dst = pl.multiple_of(t * ROW_CHUNKS, ROW_CHUNKS)
        return pltpu.make_async_copy(ys_hbm.at[pl.ds(src, ROW_CHUNKS)], bufs[k].at[pl.ds(dst, ROW_CHUNKS)], sem)

    _row_dma_all(tc, row_copy)

    g = jnp.concatenate([topg_ref[...], jnp.zeros((8 - TOP_K, tc), F32)], axis=0).T
    out = h1_ref[...]
    for k in range(TOP_K):
        out = out + g[:, k:k + 1] * _load_rows(bufs[k], tc)
    o_ref[...] = out


def _combine(dest, ys, h1, topg):
    n_tiles, per_tile = dest.shape
    tc = per_tile // TOP_K
    t, d = h1.shape
    return pl.pallas_call(
        _combine_kernel,
        grid=(n_tiles,),
        in_specs=[pl.BlockSpec(memory_space=pl.ANY), pl.BlockSpec(memory_space=pl.ANY),
                  pl.BlockSpec((tc, d), lambda i: (i, 0)), pl.BlockSpec((TOP_K, tc), lambda i: (0, i))],
        out_specs=pl.BlockSpec((tc, d), lambda i: (i, 0)),
        out_shape=jax.ShapeDtypeStruct((t, d), F32),
        scratch_shapes=[pltpu.SMEM((per_tile,), I32)] + [pltpu.VMEM((tc * ROW_CHUNKS, 128), U32)] * TOP_K
                       + [pltpu.SemaphoreType.DMA, pltpu.SemaphoreType.DMA],
        compiler_params=_cparams(("arbitrary",)),
        name="combine",
    )(dest, ys, h1, topg)


def _tile(n, pref):
    return pref if n % pref == 0 else n


def _layer(h, norm1_g, w_in, b_gates, conv_dw, conv_dw_b, conv_ln_g, conv_ln_b, w_conv_out,
           q_norm_g, k_norm_g, w_attn_out, w_out, norm2_g, w_router, b_router,
           w_up_gate, b_up_gate, w_down, b_down):
    b, s, d = h.shape
    t = b * s
    assert s % MOBA_BLOCK == 0 and s // MOBA_BLOCK <= GATE_SLOTS
    tm = _tile(s, 512)
    x2 = h.reshape(t, d)
    hglu, q, k, v, gates, gt = _inproj(x2, norm1_g, w_in, b_gates, q_norm_g, k_norm_g, s, tm)
    c = hglu.shape[1]
    aw = q.shape[1]
    conv = _conv(hglu.reshape(b, s, c), conv_dw, conv_dw_b, conv_ln_g, conv_ln_b, tm)
    attn = _attention(q.reshape(b, s, aw), k.reshape(b, s, N_HEADS * K_AUG), v.reshape(b, s, aw), gt, tm)
    h1, u2p, topi, topg = _mixout(x2, conv.reshape(t, c), attn.reshape(t, aw), gates, w_conv_out, w_attn_out,
                                  w_out, norm2_g, w_router, b_router, tm)
    n_blocks = -(-(t * TOP_K) // EXPERT_ROWS) + N_EXPERTS
    rank, cnt = _rank(topi, tm)
    dest, blke, meta = _dest(topi, rank, cnt, tm, n_blocks)
    dest = dest.reshape(t // tm, TOP_K * tm)
    meta = meta.reshape(128)
    xs = _dispatch(dest, u2p, meta, n_blocks * EXPERT_ROWS)
    ys = _experts(blke.reshape(n_blocks), meta, xs, w_up_gate, b_up_gate, w_down, b_down)
    out = _combine(dest, ys, h1, topg)
    return out.reshape(b, s, d)


def kernel(x, norm1_g, w_in, b_gates, conv_dw, conv_dw_b, conv_ln_g, conv_ln_b, w_conv_out, q_norm_g, k_norm_g,
           w_attn_out, w_out, norm2_g, w_router, b_router, w_up_gate, b_up_gate, w_down, b_down):
    h = x
    for l in range(norm1_g.shape[0]):
        h = _layer(h, norm1_g[l], w_in[l], b_gates[l], conv_dw[l], conv_dw_b[l], conv_ln_g[l], conv_ln_b[l],
                   w_conv_out[l], q_norm_g[l], k_norm_g[l], w_attn_out[l], w_out[l], norm2_g[l], w_router[l],
                   b_router[l], w_up_gate[l], b_up_gate[l], w_down[l], b_down[l])
    return h
```

```python
import functools

import jax
import jax.numpy as jnp
from jax import lax
from jax.experimental import pallas as pl
from jax.experimental.pallas import tpu as pltpu

F32 = jnp.float32
BF16 = jnp.bfloat16
I32 = jnp.int32
U32 = jnp.uint32

EPS = 1e-6
CONV_WIDTH = 31
HALO = 32
N_HEADS = 8
HEAD_DIM = 64
MOBA_BLOCK = 256
MOBA_TOPK = 3
GATE_SLOTS = 16
HEADS_PER_GROUP = 4
K_AUG = 128
AUG_POS = 16
V_ROWS = 80
MASKED = -1e30
LOG2E = 1.4426950408889634
M_INIT = -3e38
ATTN_UNROLL = 4
ATTN_LOOKAHEAD = 4
ATTN_SLOTS = 6
N_EXPERTS = 32
TOP_K = 4
SWIGLU_LIMIT = 7.0
SWIGLU_ALPHA = 1.702
EXPERT_ROWS = 512
ROW_CHUNKS = 4
DMA_UNROLL = 8
NEG_INF = float("-inf")
VMEM_LIMIT = 56 * 1024 * 1024


def _cparams(sem):
    return pltpu.CompilerParams(dimension_semantics=sem, vmem_limit_bytes=VMEM_LIMIT)


def _dot(a, b):
    return jnp.dot(a, b, preferred_element_type=F32)


def _dot_nt(a, b):
    return lax.dot_general(a, b, (((1,), (1,)), ((), ())), preferred_element_type=F32)


def _split_bf16(x):
    hi = x.astype(BF16)
    lo = (x - hi.astype(F32)).astype(BF16)
    return hi, lo


def _sigmoid(x):
    return 1.0 / (1.0 + jnp.exp(-x))


def _inproj_kernel(tiles_per_seq, x_ref, g1_ref, wglu_ref, wq_ref, wk_ref, wv_ref, wg_ref, bg_ref,
                   qg_ref, kg_ref, bd_ref, place_ref,
                   h_ref, q_ref, k_ref, v_ref, gate_ref, gt_ref, km_ref):
    i = pl.program_id(0)
    tm = x_ref.shape[0]
    c = h_ref.shape[1]
    aw = q_ref.shape[1]

    @pl.when(i == 0)
    def _():
        km_ref[...] = jnp.zeros_like(km_ref)

    x = x_ref[...]
    ms = jnp.mean(x * x, axis=-1, keepdims=True)
    u = (x * lax.rsqrt(ms + EPS) * g1_ref[...]).astype(BF16)

    glu = _dot(u, wglu_ref[...])
    h_ref[...] = (glu[:, :c] * _sigmoid(glu[:, c:])).astype(BF16)

    bd = bd_ref[...]

    def head_norm(t, g):
        ss = _dot((t * t).astype(BF16), bd)
        return t * lax.rsqrt(ss * (1.0 / HEAD_DIM) + EPS) * g

    qn = head_norm(_dot(u, wq_ref[...]), qg_ref[...]) * (HEAD_DIM ** -0.5 * LOG2E)
    kn = head_norm(_dot(u, wk_ref[...]), kg_ref[...])
    q_hi, q_lo = _split_bf16(qn)
    q_ref[...] = q_hi
    pos = lax.broadcasted_iota(I32, (tm, K_AUG), 0) + (i % tiles_per_seq) * tm
    col = lax.broadcasted_iota(I32, (tm, K_AUG), 1)
    kpos = pos & (MOBA_BLOCK - 1)
    bstart = pos - kpos
    aug = jnp.where((col >= HEAD_DIM) & (col < HEAD_DIM + 3), kpos,
                    jnp.where((col >= HEAD_DIM + 3) & (col < HEAD_DIM + 6), bstart,
                              jnp.where((col - (HEAD_DIM + AUG_POS)) * MOBA_BLOCK == bstart, 1, 0))).astype(F32)
    kp = _dot(kn.astype(BF16), place_ref[...])
    for h in range(N_HEADS):
        k_ref[:, h * K_AUG:(h + 1) * K_AUG] = (kp[:, h * K_AUG:(h + 1) * K_AUG] + aug).astype(BF16)
    v_ref[...] = _dot(u, wv_ref[...]).astype(BF16)
    gate_ref[...] = _sigmoid(_dot(u, wg_ref[...]) + bg_ref[...]).astype(BF16)

    lane_head = lax.broadcasted_iota(I32, (1, aw), 1) // HEAD_DIM
    blocks_per_tile = tm // MOBA_BLOCK
    n0 = (i % tiles_per_seq) * blocks_per_tile
    for j in range(blocks_per_tile):
        kmj = jnp.mean(kn[j * MOBA_BLOCK:(j + 1) * MOBA_BLOCK], axis=0, keepdims=True)
        for h in range(N_HEADS):
            km_ref[pl.ds(h * GATE_SLOTS + n0 + j, 1), :] = jnp.where(lane_head == h, kmj, 0.0)

    km_hi, km_lo = _split_bf16(km_ref[...])
    gt_ref[0] = _dot_nt(km_hi, q_hi) + _dot_nt(km_lo, q_hi) + _dot_nt(km_hi, q_lo)


def _inproj(x2, norm1_g, w_in, b_gates, q_norm_g, k_norm_g, seq, tm):
    t, d = x2.shape
    aw = N_HEADS * HEAD_DIM
    c = (w_in.shape[1] - 3 * aw - 2 * d) // 2
    c1, c2, c3, c4 = 2 * c, 2 * c + aw, 2 * c + 2 * aw, 2 * c + 3 * aw
    wb = w_in.astype(BF16)
    hd = jnp.arange(aw, dtype=I32) // HEAD_DIM
    bd = (hd[:, None] == hd[None, :]).astype(BF16)
    lane = jnp.arange(aw, dtype=I32)
    place = ((lane // HEAD_DIM * K_AUG + lane % HEAD_DIM)[:, None]
             == jnp.arange(N_HEADS * K_AUG, dtype=I32)[None, :]).astype(BF16)
    n_tiles = t // tm
    full = lambda a: pl.BlockSpec(a.shape, lambda i: (0,) * a.ndim)
    row = lambda n: pl.BlockSpec((tm, n), lambda i: (i, 0))
    args = (x2, norm1_g.reshape(1, d), wb[:, :c1], wb[:, c1:c2], wb[:, c2:c3], wb[:, c3:c4], wb[:, c4:],
            b_gates.reshape(1, 2 * d), jnp.tile(q_norm_g, N_HEADS).reshape(1, aw),
            jnp.tile(k_norm_g, N_HEADS).reshape(1, aw), bd, place)
    return pl.pallas_call(
        functools.partial(_inproj_kernel, seq // tm),
        grid=(n_tiles,),
        in_specs=[row(d)] + [full(a) for a in args[1:]],
        out_specs=[row(c), row(aw), row(N_HEADS * K_AUG), row(aw), row(2 * d),
                   pl.BlockSpec((1, N_HEADS * GATE_SLOTS, tm), lambda i: (i, 0, 0))],
        out_shape=[jax.ShapeDtypeStruct((t, c), BF16), jax.ShapeDtypeStruct((t, aw), BF16),
                   jax.ShapeDtypeStruct((t, N_HEADS * K_AUG), BF16), jax.ShapeDtypeStruct((t, aw), BF16),
                   jax.ShapeDtypeStruct((t, 2 * d), BF16),
                   jax.ShapeDtypeStruct((n_tiles, N_HEADS * GATE_SLOTS, tm), F32)],
        scratch_shapes=[pltpu.VMEM((N_HEADS * GATE_SLOTS, aw), F32)],
        compiler_params=_cparams(("arbitrary",)),
        name="inproj",
    )(*args)


def _conv_kernel(h_ref, dw_ref, dwb_ref, lng_ref, lnb_ref, o_ref, win_ref, sh_ref):
    s = pl.program_id(1)
    ts = h_ref.shape[1]

    @pl.when(s == 0)
    def _():
        win_ref[0:HALO, :] = jnp.zeros((HALO, win_ref.shape[1]), F32)

    @pl.when(s != 0)
    def _():
        win_ref[0:HALO, :] = win_ref[ts:ts + HALO, :]

    win_ref[HALO:HALO + ts, :] = h_ref[0].astype(F32)
    span = ts + HALO - 8
    for r in range(1, 8):
        sh_ref[r - 1] = win_ref[r:r + span, :]
    acc = jnp.zeros((ts, win_ref.shape[1]), F32) + dwb_ref[...]
    for w in range(CONV_WIDTH):
        off = HALO - (CONV_WIDTH - 1) + w
        r = off % 8
        tap = win_ref[off:off + ts, :] if r == 0 else sh_ref[r - 1, off - r:off - r + ts, :]
        acc = acc + dw_ref[w:w + 1, :] * tap
    mu = jnp.mean(acc, axis=-1, keepdims=True)
    cen = acc - mu
    var = jnp.mean(cen * cen, axis=-1, keepdims=True)
    y = cen * lax.rsqrt(var + EPS) * lng_ref[...] + lnb_ref[...]
    o_ref[0] = (y * _sigmoid(y)).astype(BF16)


def _conv(h3, dw, dwb, lng, lnb, ts):
    b, s, c = h3.shape
    vec = lambda a: pl.BlockSpec(a.shape, lambda i, j: (0, 0))
    args = (h3, dw, dwb.reshape(1, c), lng.reshape(1, c), lnb.reshape(1, c))
    return pl.pallas_call(
        _conv_kernel,
        grid=(b, s // ts),
        in_specs=[pl.BlockSpec((1, ts, c), lambda i, j: (i, j, 0))] + [vec(a) for a in args[1:]],
        out_specs=pl.BlockSpec((1, ts, c), lambda i, j: (i, j, 0)),
        out_shape=jax.ShapeDtypeStruct((b, s, c), BF16),
        scratch_shapes=[pltpu.VMEM((ts + HALO, c), F32), pltpu.VMEM((7, ts + HALO - 8, c), F32)],
        compiler_params=_cparams(("arbitrary", "arbitrary")),
        name="conv",
    )(*args)


def _attn_kernel(q_ref, k_ref, v_ref, gt_ref, srow_ref, o_ref,
                 vt_ref, caus_ref, w_ref, s_ref, cm_ref, m_ref, acc_ref):
    qb = pl.program_id(2)
    blk = MOBA_BLOCK
    hp = HEADS_PER_GROUP
    n_blocks = v_ref.shape[1] // blk

    @pl.when(qb == 0)
    def _():
        ones_rows = (lax.broadcasted_iota(I32, (V_ROWS - HEAD_DIM, blk), 0) == 0).astype(BF16)
        for n in range(n_blocks):
            vt = v_ref[0, n * blk:(n + 1) * blk, :].astype(F32).T
            for h in range(hp):
                vt_ref[n, h, 0:HEAD_DIM, :] = vt[h * HEAD_DIM:(h + 1) * HEAD_DIM].astype(BF16)
                vt_ref[n, h, HEAD_DIM:V_ROWS, :] = ones_rows
        kpos = lax.broadcasted_iota(I32, (blk, blk), 0)
        qpos = lax.broadcasted_iota(I32, (blk, blk), 1)
        caus_ref[0] = jnp.zeros((blk, blk), F32)
        caus_ref[1] = jnp.where(kpos <= qpos, 0.0, NEG_INF)
        for h in range(hp):
            w_ref[h, HEAD_DIM:HEAD_DIM + AUG_POS, :] = srow_ref[h]
            w_ref[h, HEAD_DIM + AUG_POS + GATE_SLOTS:K_AUG, :] = jnp.zeros(
                (K_AUG - HEAD_DIM - AUG_POS - GATE_SLOTS, blk), BF16)

    qt = q_ref[0].astype(F32).T
    slot = lax.broadcasted_iota(I32, (GATE_SLOTS, blk), 0)
    for h in range(hp):
        g = gt_ref[0, h * GATE_SLOTS:(h + 1) * GATE_SLOTS, :]
        g = jnp.where(slot < qb, g, NEG_INF)
        sel = slot == qb
        for _ in range(MOBA_TOPK):
            mx = jnp.max(g, axis=0, keepdims=True)
            first = jnp.min(jnp.where(g == mx, slot, GATE_SLOTS), axis=0, keepdims=True)
            pick = (slot == first) & (mx > NEG_INF)
            sel = sel | pick
            g = jnp.where(pick, NEG_INF, g)
        w_ref[h, 0:HEAD_DIM, :] = qt[h * HEAD_DIM:(h + 1) * HEAD_DIM].astype(BF16)
        w_ref[h, HEAD_DIM + AUG_POS:HEAD_DIM + AUG_POS + GATE_SLOTS, :] = jnp.where(sel, 0.0, MASKED).astype(BF16)

    m_ref[...] = jnp.full(m_ref.shape, M_INIT, F32)
    acc_ref[...] = jnp.zeros(acc_ref.shape, F32)

    def scores(n, h, slot):
        start = pl.multiple_of(n * blk, blk)
        t = _dot(k_ref[0, pl.ds(start, blk), h * K_AUG:(h + 1) * K_AUG], w_ref[h])
        t = t + caus_ref[jnp.where(n == qb, 1, 0)]
        s_ref[slot] = t
        cm_ref[slot] = jnp.max(t, axis=0, keepdims=True)

    def accumulate(n, h, slot):
        m_old = m_ref[h]
        m_new = jnp.maximum(m_old, cm_ref[slot])
        p = jnp.exp2(s_ref[slot] - m_new).astype(BF16)
        pv = _dot(vt_ref[n, h], p)
        m_ref[h] = m_new
        acc_ref[h] = jnp.exp2(m_old - m_new) * acc_ref[h] + pv

    def chunk(c, carry):
        units = [(c * ATTN_UNROLL + u, h) for u in range(ATTN_UNROLL) for h in range(hp)]
        for j in range(ATTN_LOOKAHEAD):
            scores(*units[j], j % ATTN_SLOTS)
        for j, (n, h) in enumerate(units):
            if j + ATTN_LOOKAHEAD < len(units):
                scores(*units[j + ATTN_LOOKAHEAD], (j + ATTN_LOOKAHEAD) % ATTN_SLOTS)
            accumulate(n, h, j % ATTN_SLOTS)
        return carry

    lax.fori_loop(0, (qb + ATTN_UNROLL) // ATTN_UNROLL, chunk, 0)
    outs = [acc_ref[h, 0:HEAD_DIM, :] / acc_ref[h, HEAD_DIM:HEAD_DIM + 1, :] for h in range(hp)]
    o_ref[0] = jnp.concatenate(outs, axis=0).T.astype(BF16)


def _slope_rows():
    slopes = 2.0 ** (-8.0 * jnp.arange(1, N_HEADS + 1, dtype=F32) / N_HEADS) * LOG2E
    s1 = slopes.astype(BF16)
    r1 = slopes - s1.astype(F32)
    s2 = r1.astype(BF16)
    s3 = (r1 - s2.astype(F32)).astype(BF16)
    parts = jnp.stack([s1, s2, s3, s1, s2, s3] + [jnp.zeros_like(s1)] * (AUG_POS - 6), axis=1)
    return jnp.broadcast_to(parts[:, :, None], (N_HEADS, AUG_POS, MOBA_BLOCK))


def _attention(q3, kaug3, v3, gt, tm):
    b, s, aw = q3.shape
    blk = MOBA_BLOCK
    hp = HEADS_PER_GROUP
    gw = hp * HEAD_DIM
    n_groups = aw // gw
    nb = s // blk
    assert nb % ATTN_UNROLL == 0 and nb <= GATE_SLOTS
    per_tile = tm // blk
    tiles_per_seq = s // tm
    return pl.pallas_call(
        _attn_kernel,
        grid=(b, n_groups, nb),
        in_specs=[
            pl.BlockSpec((1, blk, gw), lambda i, g, j: (i, j, g)),
            pl.BlockSpec((1, s, hp * K_AUG), lambda i, g, j: (i, 0, g)),
            pl.BlockSpec((1, s, gw), lambda i, g, j: (i, 0, g)),
            pl.BlockSpec((1, hp * GATE_SLOTS, blk),
                         lambda i, g, j: (i * tiles_per_seq + j // per_tile, g, j % per_tile)),
            pl.BlockSpec((hp, AUG_POS, blk), lambda i, g, j: (g, 0, 0)),
        ],
        out_specs=pl.BlockSpec((1, blk, gw), lambda i, g, j: (i, j, g)),
        scratch_shapes=[
            pltpu.VMEM((nb, hp, V_ROWS, blk), BF16),
            pltpu.VMEM((2, blk, blk), F32),
            pltpu.VMEM((hp, K_AUG, blk), BF16),
            pltpu.VMEM((ATTN_SLOTS, blk, blk), F32),
            pltpu.VMEM((ATTN_SLOTS, 1, blk), F32),
            pltpu.VMEM((hp, 1, blk), F32),
            pltpu.VMEM((hp, V_ROWS, blk), F32),
        ],
        out_shape=jax.ShapeDtypeStruct((b, s, aw), BF16),
        compiler_params=_cparams(("arbitrary", "arbitrary", "arbitrary")),
        name="attn",
    )(q3, kaug3, v3, gt, _slope_rows())


def _pack_bf16_pairs(x):
    n = x.shape[1] // 2
    r = x.astype(BF16).astype(F32)
    lo = lax.shift_right_logical(pltpu.bitcast(r[:, :n], U32), jnp.uint32(16))
    hi = pltpu.bitcast(r[:, n:], U32) & jnp.uint32(0xFFFF0000)
    return hi | lo


def _store_rows(ref, x):
    pk = _pack_bf16_pairs(x)
    m, n = pk.shape
    c = n // 128
    for j in range(c):
        ref[pl.ds(j, m, stride=c), :] = pk[:, j * 128:(j + 1) * 128]


def _load_rows(ref, m):
    c = ref.shape[0] // m
    ws = [ref[pl.ds(j, m, stride=c), :] for j in range(c)]
    lo = [pltpu.bitcast(lax.shift_left(w, jnp.uint32(16)), F32) for w in ws]
    hi = [pltpu.bitcast(w & jnp.uint32(0xFFFF0000), F32) for w in ws]
    return jnp.concatenate(lo + hi, axis=1)


def _mixout_kernel(x_ref, c_ref, a_ref, gate_ref, wco_ref, wao_ref, wout_ref, g2_ref,
                   wr_ref, br_ref,
                   h1_ref, u2_ref, topi_ref, topg_ref):
    d = x_ref.shape[1]
    y_c = _dot(c_ref[...], wco_ref[...])
    y_a = _dot(a_ref[...], wao_ref[...])
    gate = gate_ref[...].astype(F32)
    mixed = (gate[:, :d] * y_c + gate[:, d:] * y_a).astype(BF16)
    h1 = x_ref[...] + _dot(mixed, wout_ref[...])
    h1_ref[...] = h1
    ms = jnp.mean(h1 * h1, axis=-1, keepdims=True)
    u2 = h1 * lax.rsqrt(ms + EPS) * g2_ref[...]
    _store_rows(u2_ref, u2)
    u_hi, u_lo = _split_bf16(u2)
    wr = wr_ref[...]
    both = _dot_nt(wr, u_hi)
    logits = both[:N_EXPERTS] + both[N_EXPERTS:] + _dot_nt(wr[:N_EXPERTS], u_lo) + br_ref[...]
    eid = lax.broadcasted_iota(I32, logits.shape, 0)
    vals = []
    for k in range(TOP_K):
        mx = jnp.max(logits, axis=0, keepdims=True)
        first = jnp.min(jnp.where(logits == mx, eid, N_EXPERTS), axis=0, keepdims=True)
        topi_ref[k:k + 1, :] = first
        vals.append(mx)
        logits = jnp.where(eid == first, NEG_INF, logits)
    es = [jnp.exp(v - vals[0]) for v in vals]
    den = es[0] + es[1] + es[2] + es[3]
    for k in range(TOP_K):
        topg_ref[k:k + 1, :] = es[k] / den


def _mixout(x2, c2, a2, gates, w_conv_out, w_attn_out, w_out, norm2_g, w_router, b_router, tm):
    t, d = x2.shape
    wr_t = w_router.T
    wrh = wr_t.astype(BF16)
    wrl = (wr_t - wrh.astype(F32)).astype(BF16)
    args = (x2, c2, a2, gates, w_conv_out.astype(BF16), w_attn_out.astype(BF16), w_out.astype(BF16),
            norm2_g.reshape(1, d), jnp.concatenate([wrh, wrl], axis=0), b_router.reshape(N_EXPERTS, 1))
    full = lambda a: pl.BlockSpec(a.shape, lambda i: (0,) * a.ndim)
    row = lambda a: pl.BlockSpec((tm, a.shape[1]), lambda i: (i, 0))
    return pl.pallas_call(
        _mixout_kernel,
        grid=(t // tm,),
        in_specs=[row(a) for a in args[:4]] + [full(a) for a in args[4:]],
        out_specs=[pl.BlockSpec((tm, d), lambda i: (i, 0)), pl.BlockSpec((tm * ROW_CHUNKS, 128), lambda i: (i, 0)),
                   pl.BlockSpec((TOP_K, tm), lambda i: (0, i)), pl.BlockSpec((TOP_K, tm), lambda i: (0, i))],
        out_shape=[jax.ShapeDtypeStruct((t, d), F32), jax.ShapeDtypeStruct((t * ROW_CHUNKS, 128), U32),
                   jax.ShapeDtypeStruct((TOP_K, t), I32), jax.ShapeDtypeStruct((TOP_K, t), F32)],
        compiler_params=_cparams(("arbitrary",)),
        name="mixout",
    )(*args)


def _rank_kernel(topi_ref, tri_ref, rank_ref, cnt_ref, run_ref):
    i = pl.program_id(0)
    tr = topi_ref.shape[1]

    @pl.when(i == 0)
    def _():
        run_ref[...] = jnp.zeros_like(run_ref)

    eid = lax.broadcasted_iota(I32, (N_EXPERTS, tr), 0)
    onehots = [(eid == topi_ref[k:k + 1, :]).astype(F32) for k in range(TOP_K)]
    tot = onehots[0] + onehots[1] + onehots[2] + onehots[3]
    before = _dot(tot.astype(BF16), tri_ref[...])
    base = run_ref[...] + before
    for k in range(TOP_K):
        rank_ref[k:k + 1, :] = jnp.sum(onehots[k] * base, axis=0, keepdims=True).astype(I32)
    run_ref[...] = run_ref[...] + jnp.sum(tot, axis=1, keepdims=True)
    cnt_ref[...] = run_ref[...]


def _rank(topi, tr):
    t = topi.shape[1]
    tri = (jnp.arange(tr)[:, None] < jnp.arange(tr)[None, :]).astype(BF16)
    return pl.pallas_call(
        _rank_kernel,
        grid=(t // tr,),
        in_specs=[pl.BlockSpec((TOP_K, tr), lambda i: (0, i)), pl.BlockSpec((tr, tr), lambda i: (0, 0))],
        out_specs=[pl.BlockSpec((TOP_K, tr), lambda i: (0, i)), pl.BlockSpec((N_EXPERTS, 1), lambda i: (0, 0))],
        out_shape=[jax.ShapeDtypeStruct((TOP_K, t), I32), jax.ShapeDtypeStruct((N_EXPERTS, 1), F32)],
        scratch_shapes=[pltpu.VMEM((N_EXPERTS, 1), F32)],
        compiler_params=_cparams(("arbitrary",)),
        name="rank",
    )(topi, tri)


def _dest_kernel(n_blocks, topi_ref, rank_ref, cnt_ref, dest_ref, blke_ref, meta_ref):
    td = topi_ref.shape[1]
    cnt = cnt_ref[...]
    padded = jnp.floor((cnt + (EXPERT_ROWS - 1)) * (1.0 / EXPERT_ROWS)) * EXPERT_ROWS
    er = lax.broadcasted_iota(I32, (N_EXPERTS, N_EXPERTS), 0)
    ec = lax.broadcasted_iota(I32, (N_EXPERTS, N_EXPERTS), 1)
    pad_row = jnp.sum(jnp.where(er == ec, padded, 0.0), axis=0, keepdims=True)
    start = jnp.sum(jnp.where(ec < er, pad_row, 0.0), axis=1, keepdims=True)
    end = start + padded
    eid = lax.broadcasted_iota(I32, (N_EXPERTS, td), 0)
    for k in range(TOP_K):
        oh = eid == topi_ref[k:k + 1, :]
        s_k = jnp.sum(jnp.where(oh, start, 0.0), axis=0, keepdims=True)
        dest_ref[0, k:k + 1, :] = (s_k.astype(I32) + rank_ref[k:k + 1, :]) * ROW_CHUNKS
    bstart = (lax.broadcasted_iota(I32, (N_EXPERTS, n_blocks), 1) * EXPERT_ROWS).astype(F32)
    be = jnp.sum((end <= bstart).astype(I32), axis=0, keepdims=True)
    blke_ref[...] = jnp.minimum(be, N_EXPERTS - 1)
    er2 = lax.broadcasted_iota(I32, (N_EXPERTS, 128), 0)
    lane = lax.broadcasted_iota(I32, (N_EXPERTS, 128), 1)
    last = jnp.maximum(end * (1.0 / EXPERT_ROWS) - 1.0, 0.0)
    used = jnp.where(er2 == N_EXPERTS - 1, end * (1.0 / EXPERT_ROWS), 0.0)
    meta = jnp.where(lane == er2, last, 0.0) + jnp.where(lane == N_EXPERTS, used, 0.0)
    meta_ref[...] = jnp.sum(meta, axis=0, keepdims=True).astype(I32)


def _dest(topi, rank, cnt, td, n_blocks):
    t = topi.shape[1]
    return pl.pallas_call(
        functools.partial(_dest_kernel, n_blocks),
        grid=(t // td,),
        in_specs=[pl.BlockSpec((TOP_K, td), lambda i: (0, i)), pl.BlockSpec((TOP_K, td), lambda i: (0, i)),
                  pl.BlockSpec((N_EXPERTS, 1), lambda i: (0, 0))],
        out_specs=[pl.BlockSpec((1, TOP_K, td), lambda i: (i, 0, 0)),
                   pl.BlockSpec((1, n_blocks), lambda i: (0, 0)), pl.BlockSpec((1, 128), lambda i: (0, 0))],
        out_shape=[jax.ShapeDtypeStruct((t // td, TOP_K, td), I32), jax.ShapeDtypeStruct((1, n_blocks), I32),
                   jax.ShapeDtypeStruct((1, 128), I32)],
        compiler_params=_cparams(("arbitrary",)),
        name="dest",
    )(topi, rank, cnt)


def _row_dma_start(n_tok, row_copy):
    def issue(it, carry):
        for u in range(DMA_UNROLL):
            for k in range(TOP_K):
                row_copy(it * DMA_UNROLL + u, k).start(priority=k % 2)
        return carry

    lax.fori_loop(0, n_tok // DMA_UNROLL, issue, 0)


def _row_dma_wait(n_tok, row_copy):
    def drain(it, carry):
        for u in range(DMA_UNROLL):
            for k in range(TOP_K):
                row_copy(it * DMA_UNROLL + u, k).wait()
        return carry

    lax.fori_loop(0, n_tok // DMA_UNROLL, drain, 0)


def _load_dest(dest_hbm, tile, dest_smem, sem_idx):
    cp = pltpu.make_async_copy(dest_hbm.at[tile], dest_smem, sem_idx)
    cp.start()
    cp.wait()


def _dispatch_kernel(dest_hbm, u2_ref, xs_in, xs_hbm, dest_smem, sem_idx, sem):
    del xs_in
    td = u2_ref.shape[0] // ROW_CHUNKS
    _load_dest(dest_hbm, pl.program_id(0), dest_smem, sem_idx)

    def row_copy(t, k):
        src = pl.multiple_of(t * ROW_CHUNKS, ROW_CHUNKS)
        dst = pl.multiple_of(dest_smem[k * td + t], ROW_CHUNKS)
        return pltpu.make_async_copy(u2_ref.at[pl.ds(src, ROW_CHUNKS)], xs_hbm.at[pl.ds(dst, ROW_CHUNKS)], sem)

    _row_dma_start(td, row_copy)
    _row_dma_wait(td, row_copy)


def _zfill_kernel(meta_ref, o_ref):
    del meta_ref
    o_ref[...] = jnp.zeros(o_ref.shape, U32)


def _dispatch(dest, u2p, meta, n_rows):
    n_tiles, per_tile = dest.shape
    td = per_tile // TOP_K
    blk_rows = EXPERT_ROWS * ROW_CHUNKS
    xs0 = pl.pallas_call(
        _zfill_kernel,
        grid_spec=pltpu.PrefetchScalarGridSpec(
            num_scalar_prefetch=1, grid=(N_EXPERTS,), in_specs=[],
            out_specs=pl.BlockSpec((blk_rows, 128), lambda e, mt: (mt[e], 0))),
        out_shape=jax.ShapeDtypeStruct((n_rows * ROW_CHUNKS, 128), U32),
        compiler_params=_cparams(("arbitrary",)),
        name="zfill",
    )(meta)
    return pl.pallas_call(
        _dispatch_kernel,
        grid=(n_tiles,),
        in_specs=[pl.BlockSpec(memory_space=pl.ANY), pl.BlockSpec((td * ROW_CHUNKS, 128), lambda i: (i, 0)),
                  pl.BlockSpec(memory_space=pl.ANY)],
        out_specs=pl.BlockSpec(memory_space=pl.ANY),
        out_shape=jax.ShapeDtypeStruct(xs0.shape, U32),
        scratch_shapes=[pltpu.SMEM((per_tile,), I32), pltpu.SemaphoreType.DMA, pltpu.SemaphoreType.DMA],
        input_output_aliases={2: 0},
        compiler_params=_cparams(("arbitrary",)),
        name="dispatch",
    )(dest, u2p, xs0)


def _used_block(i, meta_ref):
    return jnp.minimum(i, meta_ref[N_EXPERTS] - 1)


def _expert_kernel(blke_ref, meta_ref, xs_ref, wug_ref, bug_ref, wd_ref, bd_ref, ys_ref, wug_bf, wd_bf):
    i = pl.program_id(0)
    f = wd_ref.shape[1]
    cur = blke_ref[_used_block(i, meta_ref)]
    prev = blke_ref[_used_block(jnp.maximum(i - 1, 0), meta_ref)]

    @pl.when((i == 0) | (cur != prev))
    def _():
        wug_bf[...] = wug_ref[0].astype(BF16)
        wd_bf[...] = wd_ref[0].astype(BF16)

    @pl.when(i < meta_ref[N_EXPERTS])
    def _():
        x = _load_rows(xs_ref, EXPERT_ROWS).astype(BF16)
        h = _dot(x, wug_bf[...]) + bug_ref[0]
        a = jnp.minimum(h[:, :f], SWIGLU_LIMIT)
        lin = jnp.clip(h[:, f:], -SWIGLU_LIMIT, SWIGLU_LIMIT)
        act = (a * _sigmoid(SWIGLU_ALPHA * a) * (lin + 1.0)).astype(BF16)
        y = _dot(act, wd_bf[...]) + bd_ref[0]
        _store_rows(ys_ref, y)


def _experts(blke, meta, xs, w_up_gate, b_up_gate, w_down, b_down):
    e, d, f2 = w_up_gate.shape
    blk_rows = EXPERT_ROWS * ROW_CHUNKS
    n_blocks = xs.shape[0] // blk_rows
    row_blk = lambda i, be, mt: (_used_block(i, mt), 0)
    expert = lambda i, be, mt: (be[_used_block(i, mt)], 0, 0)
    return pl.pallas_call(
        _expert_kernel,
        grid_spec=pltpu.PrefetchScalarGridSpec(
            num_scalar_prefetch=2,
            grid=(n_blocks,),
            in_specs=[
                pl.BlockSpec((blk_rows, 128), row_blk),
                pl.BlockSpec((1, d, f2), expert),
                pl.BlockSpec((1, 1, f2), expert),
                pl.BlockSpec((1, f2 // 2, d), expert),
                pl.BlockSpec((1, 1, d), expert),
            ],
            out_specs=pl.BlockSpec((blk_rows, 128), row_blk),
            scratch_shapes=[pltpu.VMEM((d, f2), BF16), pltpu.VMEM((f2 // 2, d), BF16)],
        ),
        out_shape=jax.ShapeDtypeStruct(xs.shape, U32),
        compiler_params=_cparams(("arbitrary",)),
        name="experts",
    )(blke, meta, xs, w_up_gate, b_up_gate.reshape(e, 1, f2), w_down, b_down.reshape(e, 1, d))


def _combine_kernel(dest_hbm, ys_hbm, h1_ref, topg_ref, o_ref, *scratch):
    i = pl.program_id(0)
    n_tiles = pl.num_programs(0) - 1
    tc = h1_ref.shape[0]
    dest_smem = scratch[0:2]
    bufs = (scratch[2:2 + TOP_K], scratch[2 + TOP_K:2 + 2 * TOP_K])
    sem_idx, sems = scratch[2 + 2 * TOP_K], scratch[3 + 2 * TOP_K]

    def row_copy(slot):
        def make(t, k):
            src = pl.multiple_of(dest_smem[slot][k * tc + t], ROW_CHUNKS)
            dst = pl.multiple_of(t * ROW_CHUNKS, ROW_CHUNKS)
            return pltpu.make_async_copy(ys_hbm.at[pl.ds(src, ROW_CHUNKS)],
                                         bufs[slot][k].at[pl.ds(dst, ROW_CHUNKS)], sems.at[slot])
        return make

    def start(slot):
        _load_dest(dest_hbm, i, dest_smem[slot], sem_idx)
        _row_dma_start(tc, row_copy(slot))

    def finish(slot):
        _row_dma_wait(tc, row_copy(slot))
        g = jnp.concatenate([topg_ref[...], jnp.zeros((8 - TOP_K, tc), F32)], axis=0).T
        out = h1_ref[...]
        for k in range(TOP_K):
            out = out + g[:, k:k + 1] * _load_rows(bufs[slot][k], tc)
        o_ref[...] = out

    for slot in range(2):
        @pl.when((i % 2 == slot) & (i < n_tiles))
        def _():
            start(slot)

    for slot in range(2):
        @pl.when((i % 2 != slot) & (i > 0))
        def _():
            finish(slot)


def _combine(dest, ys, h1, topg):
    n_tiles, per_tile = dest.shape
    tc = per_tile // TOP_K
    t, d = h1.shape
    prev = lambda i: jnp.maximum(i - 1, 0)
    return pl.pallas_call(
        _combine_kernel,
        grid=(n_tiles + 1,),
        in_specs=[pl.BlockSpec(memory_space=pl.ANY), pl.BlockSpec(memory_space=pl.ANY),
                  pl.BlockSpec((tc, d), lambda i: (prev(i), 0)), pl.BlockSpec((TOP_K, tc), lambda i: (0, prev(i)))],
        out_specs=pl.BlockSpec((tc, d), lambda i: (prev(i), 0)),
        out_shape=jax.ShapeDtypeStruct((t, d), F32),
        scratch_shapes=[pltpu.SMEM((per_tile,), I32)] * 2 + [pltpu.VMEM((tc * ROW_CHUNKS, 128), U32)] * (2 * TOP_K)
                       + [pltpu.SemaphoreType.DMA, pltpu.SemaphoreType.DMA((2,))],
        compiler_params=_cparams(("arbitrary",)),
        name="combine",
    )(dest, ys, h1, topg)


def _tile(n, pref):
    return pref if n % pref == 0 else n


def _layer(h, norm1_g, w_in, b_gates, conv_dw, conv_dw_b, conv_ln_g, conv_ln_b, w_conv_out,
           q_norm_g, k_norm_g, w_attn_out, w_out, norm2_g, w_router, b_router,
           w_up_gate, b_up_gate, w_down, b_down):
    b, s, d = h.shape
    t = b * s
    assert s % MOBA_BLOCK == 0 and s // MOBA_BLOCK <= GATE_SLOTS
    tm = _tile(s, 512)
    x2 = h.reshape(t, d)
    hglu, q, k, v, gates, gt = _inproj(x2, norm1_g, w_in, b_gates, q_norm_g, k_norm_g, s, tm)
    c = hglu.shape[1]
    aw = q.shape[1]
    conv = _conv(hglu.reshape(b, s, c), conv_dw, conv_dw_b, conv_ln_g, conv_ln_b, tm)
    attn = _attention(q.reshape(b, s, aw), k.reshape(b, s, N_HEADS * K_AUG), v.reshape(b, s, aw), gt, tm)
    h1, u2p, topi, topg = _mixout(x2, conv.reshape(t, c), attn.reshape(t, aw), gates, w_conv_out, w_attn_out,
                                  w_out, norm2_g, w_router, b_router, tm)
    n_blocks = -(-(t * TOP_K) // EXPERT_ROWS) + N_EXPERTS
    rank, cnt = _rank(topi, tm)
    dest, blke, meta = _dest(topi, rank, cnt, tm, n_blocks)
    dest = dest.reshape(t // tm, TOP_K * tm)
    meta = meta.reshape(128)
    xs = _dispatch(dest, u2p, meta, n_blocks * EXPERT_ROWS)
    ys = _experts(blke.reshape(n_blocks), meta, xs, w_up_gate, b_up_gate, w_down, b_down)
    out = _combine(dest, ys, h1, topg)
    return out.reshape(b, s, d)


def kernel(x, norm1_g, w_in, b_gates, conv_dw, conv_dw_b, conv_ln_g, conv_ln_b, w_conv_out, q_norm_g, k_norm_g,
           w_attn_out, w_out, norm2_g, w_router, b_router, w_up_gate, b_up_gate, w_down, b_down):
    h = x
    for l in range(norm1_g.shape[0]):
        h = _layer(h, norm1_g[l], w_in[l], b_gates[l], conv_dw[l], conv_dw_b[l], conv_ln_g[l], conv_ln_b[l],
                   w_conv_out[l], q_norm_g[l], k_norm_g[l], w_attn_out[l], w_out[l], norm2_g[l], w_router[l],
                   b_router[l], w_up_gate[l], b_up_gate[l], w_down[l], b_down[l])
    return h
```

```python
import functools

import jax
import jax.numpy as jnp
from jax import lax
from jax.experimental import pallas as pl
from jax.experimental.pallas import tpu as pltpu

F32 = jnp.float32
BF16 = jnp.bfloat16
I32 = jnp.int32
U32 = jnp.uint32

EPS = 1e-6
CONV_WIDTH = 31
HALO = 32
N_HEADS = 8
HEAD_DIM = 64
MOBA_BLOCK = 256
MOBA_TOPK = 3
GATE_SLOTS = 16
HEADS_PER_GROUP = 4
K_AUG = 128
AUG_POS = 16
V_ROWS = 80
MASKED = -1e30
LOG2E = 1.4426950408889634
M_INIT = -3e38
ATTN_UNROLL = 4
ATTN_LOOKAHEAD = 4
ATTN_SLOTS = 6
N_EXPERTS = 32
TOP_K = 4
SWIGLU_LIMIT = 7.0
SWIGLU_ALPHA = 1.702
EXPERT_ROWS = 512
ROW_CHUNKS = 4
DMA_UNROLL = 8
NEG_INF = float("-inf")
VMEM_LIMIT = 56 * 1024 * 1024


def _cparams(sem):
    return pltpu.CompilerParams(dimension_semantics=sem, vmem_limit_bytes=VMEM_LIMIT)


def _dot(a, b):
    return jnp.dot(a, b, preferred_element_type=F32)


def _dot_nt(a, b):
    return lax.dot_general(a, b, (((1,), (1,)), ((), ())), preferred_element_type=F32)


def _split_bf16(x):
    hi = x.astype(BF16)
    lo = (x - hi.astype(F32)).astype(BF16)
    return hi, lo


def _sigmoid(x):
    return 1.0 / (1.0 + jnp.exp(-x))


def _inproj_kernel(tiles_per_seq, x_ref, g1_ref, wglu_ref, wq_ref, wk_ref, wv_ref, wg_ref, bg_ref,
                   qg_ref, kg_ref, bd_ref, place_ref,
                   h_ref, q_ref, k_ref, v_ref, gate_ref, gt_ref, km_ref):
    i = pl.program_id(0)
    tm = x_ref.shape[0]
    c = h_ref.shape[1]
    aw = q_ref.shape[1]

    @pl.when(i == 0)
    def _():
        km_ref[...] = jnp.zeros_like(km_ref)

    x = x_ref[...]
    ms = jnp.mean(x * x, axis=-1, keepdims=True)
    u = (x * lax.rsqrt(ms + EPS) * g1_ref[...]).astype(BF16)

    glu = _dot(u, wglu_ref[...])
    h_ref[...] = (glu[:, :c] * _sigmoid(glu[:, c:])).astype(BF16)

    bd = bd_ref[...]

    def head_norm(t, g):
        ss = _dot((t * t).astype(BF16), bd)
        return t * lax.rsqrt(ss * (1.0 / HEAD_DIM) + EPS) * g

    qn = head_norm(_dot(u, wq_ref[...]), qg_ref[...]) * (HEAD_DIM ** -0.5 * LOG2E)
    kn = head_norm(_dot(u, wk_ref[...]), kg_ref[...])
    q_hi, q_lo = _split_bf16(qn)
    q_ref[...] = q_hi
    pos = lax.broadcasted_iota(I32, (tm, K_AUG), 0) + (i % tiles_per_seq) * tm
    col = lax.broadcasted_iota(I32, (tm, K_AUG), 1)
    kpos = pos & (MOBA_BLOCK - 1)
    bstart = pos - kpos
    aug = jnp.where((col >= HEAD_DIM) & (col < HEAD_DIM + 3), kpos,
                    jnp.where((col >= HEAD_DIM + 3) & (col < HEAD_DIM + 6), bstart,
                              jnp.where((col - (HEAD_DIM + AUG_POS)) * MOBA_BLOCK == bstart, 1, 0))).astype(F32)
    kp = _dot(kn.astype(BF16), place_ref[...])
    for h in range(N_HEADS):
        k_ref[:, h * K_AUG:(h + 1) * K_AUG] = (kp[:, h * K_AUG:(h + 1) * K_AUG] + aug).astype(BF16)
    v_ref[...] = _dot(u, wv_ref[...]).astype(BF16)
    gate_ref[...] = _sigmoid(_dot(u, wg_ref[...]) + bg_ref[...]).astype(BF16)

    lane_head = lax.broadcasted_iota(I32, (1, aw), 1) // HEAD_DIM
    blocks_per_tile = tm // MOBA_BLOCK
    n0 = (i % tiles_per_seq) * blocks_per_tile
    for j in range(blocks_per_tile):
        kmj = jnp.mean(kn[j * MOBA_BLOCK:(j + 1) * MOBA_BLOCK], axis=0, keepdims=True)
        for h in range(N_HEADS):
            km_ref[pl.ds(h * GATE_SLOTS + n0 + j, 1), :] = jnp.where(lane_head == h, kmj, 0.0)

    km_hi, km_lo = _split_bf16(km_ref[...])
    gt_ref[0] = _dot_nt(km_hi, q_hi) + _dot_nt(km_lo, q_hi) + _dot_nt(km_hi, q_lo)


def _inproj(x2, norm1_g, w_in, b_gates, q_norm_g, k_norm_g, seq, tm):
    t, d = x2.shape
    aw = N_HEADS * HEAD_DIM
    c = (w_in.shape[1] - 3 * aw - 2 * d) // 2
    c1, c2, c3, c4 = 2 * c, 2 * c + aw, 2 * c + 2 * aw, 2 * c + 3 * aw
    wb = w_in.astype(BF16)
    hd = jnp.arange(aw, dtype=I32) // HEAD_DIM
    bd = (hd[:, None] == hd[None, :]).astype(BF16)
    lane = jnp.arange(aw, dtype=I32)
    place = ((lane // HEAD_DIM * K_AUG + lane % HEAD_DIM)[:, None]
             == jnp.arange(N_HEADS * K_AUG, dtype=I32)[None, :]).astype(BF16)
    n_tiles = t // tm
    full = lambda a: pl.BlockSpec(a.shape, lambda i: (0,) * a.ndim)
    row = lambda n: pl.BlockSpec((tm, n), lambda i: (i, 0))
    args = (x2, norm1_g.reshape(1, d), wb[:, :c1], wb[:, c1:c2], wb[:, c2:c3], wb[:, c3:c4], wb[:, c4:],
            b_gates.reshape(1, 2 * d), jnp.tile(q_norm_g, N_HEADS).reshape(1, aw),
            jnp.tile(k_norm_g, N_HEADS).reshape(1, aw), bd, place)
    return pl.pallas_call(
        functools.partial(_inproj_kernel, seq // tm),
        grid=(n_tiles,),
        in_specs=[row(d)] + [full(a) for a in args[1:]],
        out_specs=[row(c), row(aw), row(N_HEADS * K_AUG), row(aw), row(2 * d),
                   pl.BlockSpec((1, N_HEADS * GATE_SLOTS, tm), lambda i: (i, 0, 0))],
        out_shape=[jax.ShapeDtypeStruct((t, c), BF16), jax.ShapeDtypeStruct((t, aw), BF16),
                   jax.ShapeDtypeStruct((t, N_HEADS * K_AUG), BF16), jax.ShapeDtypeStruct((t, aw), BF16),
                   jax.ShapeDtypeStruct((t, 2 * d), BF16),
                   jax.ShapeDtypeStruct((n_tiles, N_HEADS * GATE_SLOTS, tm), F32)],
        scratch_shapes=[pltpu.VMEM((N_HEADS * GATE_SLOTS, aw), F32)],
        compiler_params=_cparams(("arbitrary",)),
        name="inproj",
    )(*args)


def _conv_kernel(h_ref, dw_ref, dwb_ref, lng_ref, lnb_ref, o_ref, win_ref, sh_ref):
    s = pl.program_id(1)
    ts = h_ref.shape[1]

    @pl.when(s == 0)
    def _():
        win_ref[0:HALO, :] = jnp.zeros((HALO, win_ref.shape[1]), F32)

    @pl.when(s != 0)
    def _():
        win_ref[0:HALO, :] = win_ref[ts:ts + HALO, :]

    win_ref[HALO:HALO + ts, :] = h_ref[0].astype(F32)
    span = ts + HALO - 8
    for r in range(1, 8):
        sh_ref[r - 1] = win_ref[r:r + span, :]
    acc = jnp.zeros((ts, win_ref.shape[1]), F32) + dwb_ref[...]
    for w in range(CONV_WIDTH):
        off = HALO - (CONV_WIDTH - 1) + w
        r = off % 8
        tap = win_ref[off:off + ts, :] if r == 0 else sh_ref[r - 1, off - r:off - r + ts, :]
        acc = acc + dw_ref[w:w + 1, :] * tap
    mu = jnp.mean(acc, axis=-1, keepdims=True)
    cen = acc - mu
    var = jnp.mean(cen * cen, axis=-1, keepdims=True)
    y = cen * lax.rsqrt(var + EPS) * lng_ref[...] + lnb_ref[...]
    o_ref[0] = (y * _sigmoid(y)).astype(BF16)


def _conv(h3, dw, dwb, lng, lnb, ts):
    b, s, c = h3.shape
    vec = lambda a: pl.BlockSpec(a.shape, lambda i, j: (0, 0))
    args = (h3, dw, dwb.reshape(1, c), lng.reshape(1, c), lnb.reshape(1, c))
    return pl.pallas_call(
        _conv_kernel,
        grid=(b, s // ts),
        in_specs=[pl.BlockSpec((1, ts, c), lambda i, j: (i, j, 0))] + [vec(a) for a in args[1:]],
        out_specs=pl.BlockSpec((1, ts, c), lambda i, j: (i, j, 0)),
        out_shape=jax.ShapeDtypeStruct((b, s, c), BF16),
        scratch_shapes=[pltpu.VMEM((ts + HALO, c), F32), pltpu.VMEM((7, ts + HALO - 8, c), F32)],
        compiler_params=_cparams(("arbitrary", "arbitrary")),
        name="conv",
    )(*args)


def _attn_kernel(q_ref, k_ref, v_ref, gt_ref, srow_ref, o_ref,
                 vt_ref, caus_ref, w_ref, s_ref, cm_ref, m_ref, acc_ref):
    qb = pl.program_id(2)
    blk = MOBA_BLOCK
    hp = HEADS_PER_GROUP
    n_blocks = v_ref.shape[1] // blk

    @pl.when(qb == 0)
    def _():
        ones_rows = (lax.broadcasted_iota(I32, (V_ROWS - HEAD_DIM, blk), 0) == 0).astype(BF16)
        for n in range(n_blocks):
            vt = v_ref[0, n * blk:(n + 1) * blk, :].astype(F32).T
            for h in range(hp):
                vt_ref[n, h, 0:HEAD_DIM, :] = vt[h * HEAD_DIM:(h + 1) * HEAD_DIM].astype(BF16)
                vt_ref[n, h, HEAD_DIM:V_ROWS, :] = ones_rows
        kpos = lax.broadcasted_iota(I32, (blk, blk), 0)
        qpos = lax.broadcasted_iota(I32, (blk, blk), 1)
        caus_ref[0] = jnp.zeros((blk, blk), F32)
        caus_ref[1] = jnp.where(kpos <= qpos, 0.0, NEG_INF)
        for h in range(hp):
            w_ref[h, HEAD_DIM:HEAD_DIM + AUG_POS, :] = srow_ref[h]
            w_ref[h, HEAD_DIM + AUG_POS + GATE_SLOTS:K_AUG, :] = jnp.zeros(
                (K_AUG - HEAD_DIM - AUG_POS - GATE_SLOTS, blk), BF16)

    qt = q_ref[0].astype(F32).T
    slot = lax.broadcasted_iota(I32, (GATE_SLOTS, blk), 0)
    for h in range(hp):
        g = gt_ref[0, h * GATE_SLOTS:(h + 1) * GATE_SLOTS, :]
        g = jnp.where(slot < qb, g, NEG_INF)
        sel = slot == qb
        for _ in range(MOBA_TOPK):
            mx = jnp.max(g, axis=0, keepdims=True)
            first = jnp.min(jnp.where(g == mx, slot, GATE_SLOTS), axis=0, keepdims=True)
            pick = (slot == first) & (mx > NEG_INF)
            sel = sel | pick
            g = jnp.where(pick, NEG_INF, g)
        w_ref[h, 0:HEAD_DIM, :] = qt[h * HEAD_DIM:(h + 1) * HEAD_DIM].astype(BF16)
        w_ref[h, HEAD_DIM + AUG_POS:HEAD_DIM + AUG_POS + GATE_SLOTS, :] = jnp.where(sel, 0.0, MASKED).astype(BF16)

    m_ref[...] = jnp.full(m_ref.shape, M_INIT, F32)
    acc_ref[...] = jnp.zeros(acc_ref.shape, F32)

    def scores(n, h, slot):
        start = pl.multiple_of(n * blk, blk)
        t = _dot(k_ref[0, pl.ds(start, blk), h * K_AUG:(h + 1) * K_AUG], w_ref[h])
        t = t + caus_ref[jnp.where(n == qb, 1, 0)]
        s_ref[slot] = t
        cm_ref[slot] = jnp.max(t, axis=0, keepdims=True)

    def accumulate(n, h, slot):
        m_old = m_ref[h]
        m_new = jnp.maximum(m_old, cm_ref[slot])
        p = jnp.exp2(s_ref[slot] - m_new).astype(BF16)
        pv = _dot(vt_ref[n, h], p)
        m_ref[h] = m_new
        acc_ref[h] = jnp.exp2(m_old - m_new) * acc_ref[h] + pv

    def chunk(c, carry):
        units = [(c * ATTN_UNROLL + u, h) for u in range(ATTN_UNROLL) for h in range(hp)]
        for j in range(ATTN_LOOKAHEAD):
            scores(*units[j], j % ATTN_SLOTS)
        for j, (n, h) in enumerate(units):
            if j + ATTN_LOOKAHEAD < len(units):
                scores(*units[j + ATTN_LOOKAHEAD], (j + ATTN_LOOKAHEAD) % ATTN_SLOTS)
            accumulate(n, h, j % ATTN_SLOTS)
        return carry

    lax.fori_loop(0, (qb + ATTN_UNROLL) // ATTN_UNROLL, chunk, 0)
    outs = [acc_ref[h, 0:HEAD_DIM, :] / acc_ref[h, HEAD_DIM:HEAD_DIM + 1, :] for h in range(hp)]
    o_ref[0] = jnp.concatenate(outs, axis=0).T.astype(BF16)


def _slope_rows():
    slopes = 2.0 ** (-8.0 * jnp.arange(1, N_HEADS + 1, dtype=F32) / N_HEADS) * LOG2E
    s1 = slopes.astype(BF16)
    r1 = slopes - s1.astype(F32)
    s2 = r1.astype(BF16)
    s3 = (r1 - s2.astype(F32)).astype(BF16)
    parts = jnp.stack([s1, s2, s3, s1, s2, s3] + [jnp.zeros_like(s1)] * (AUG_POS - 6), axis=1)
    return jnp.broadcast_to(parts[:, :, None], (N_HEADS, AUG_POS, MOBA_BLOCK))


def _attention(q3, kaug3, v3, gt, tm):
    b, s, aw = q3.shape
    blk = MOBA_BLOCK
    hp = HEADS_PER_GROUP
    gw = hp * HEAD_DIM
    n_groups = aw // gw
    nb = s // blk
    assert nb % ATTN_UNROLL == 0 and nb <= GATE_SLOTS
    per_tile = tm // blk
    tiles_per_seq = s // tm
    return pl.pallas_call(
        _attn_kernel,
        grid=(b, n_groups, nb),
        in_specs=[
            pl.BlockSpec((1, blk, gw), lambda i, g, j: (i, j, g)),
            pl.BlockSpec((1, s, hp * K_AUG), lambda i, g, j: (i, 0, g)),
            pl.BlockSpec((1, s, gw), lambda i, g, j: (i, 0, g)),
            pl.BlockSpec((1, hp * GATE_SLOTS, blk),
                         lambda i, g, j: (i * tiles_per_seq + j // per_tile, g, j % per_tile)),
            pl.BlockSpec((hp, AUG_POS, blk), lambda i, g, j: (g, 0, 0)),
        ],
        out_specs=pl.BlockSpec((1, blk, gw), lambda i, g, j: (i, j, g)),
        scratch_shapes=[
            pltpu.VMEM((nb, hp, V_ROWS, blk), BF16),
            pltpu.VMEM((2, blk, blk), F32),
            pltpu.VMEM((hp, K_AUG, blk), BF16),
            pltpu.VMEM((ATTN_SLOTS, blk, blk), F32),
            pltpu.VMEM((ATTN_SLOTS, 1, blk), F32),
            pltpu.VMEM((hp, 1, blk), F32),
            pltpu.VMEM((hp, V_ROWS, blk), F32),
        ],
        out_shape=jax.ShapeDtypeStruct((b, s, aw), BF16),
        compiler_params=_cparams(("arbitrary", "arbitrary", "arbitrary")),
        name="attn",
    )(q3, kaug3, v3, gt, _slope_rows())


def _pack_bf16_pairs(x):
    n = x.shape[1] // 2
    r = x.astype(BF16).astype(F32)
    lo = lax.shift_right_logical(pltpu.bitcast(r[:, :n], U32), jnp.uint32(16))
    hi = pltpu.bitcast(r[:, n:], U32) & jnp.uint32(0xFFFF0000)
    return hi | lo


def _store_rows(ref, x):
    pk = _pack_bf16_pairs(x)
    m, n = pk.shape
    c = n // 128
    for j in range(c):
        ref[pl.ds(j, m, stride=c), :] = pk[:, j * 128:(j + 1) * 128]


def _load_rows(ref, m):
    c = ref.shape[0] // m
    ws = [ref[pl.ds(j, m, stride=c), :] for j in range(c)]
    lo = [pltpu.bitcast(lax.shift_left(w, jnp.uint32(16)), F32) for w in ws]
    hi = [pltpu.bitcast(w & jnp.uint32(0xFFFF0000), F32) for w in ws]
    return jnp.concatenate(lo + hi, axis=1)


def _mixout_kernel(x_ref, c_ref, a_ref, gate_ref, wco_ref, wao_ref, wout_ref, g2_ref,
                   wr_ref, br_ref,
                   h1_ref, u2_ref, topi_ref, topg_ref):
    d = x_ref.shape[1]
    y_c = _dot(c_ref[...], wco_ref[...])
    y_a = _dot(a_ref[...], wao_ref[...])
    gate = gate_ref[...].astype(F32)
    mixed = (gate[:, :d] * y_c + gate[:, d:] * y_a).astype(BF16)
    h1 = x_ref[...] + _dot(mixed, wout_ref[...])
    h1_ref[...] = h1
    ms = jnp.mean(h1 * h1, axis=-1, keepdims=True)
    u2 = h1 * lax.rsqrt(ms + EPS) * g2_ref[...]
    _store_rows(u2_ref, u2)
    u_hi, u_lo = _split_bf16(u2)
    wr = wr_ref[...]
    both = _dot_nt(wr, u_hi)
    logits = both[:N_EXPERTS] + both[N_EXPERTS:] + _dot_nt(wr[:N_EXPERTS], u_lo) + br_ref[...]
    eid = lax.broadcasted_iota(I32, logits.shape, 0)
    vals = []
    for k in range(TOP_K):
        mx = jnp.max(logits, axis=0, keepdims=True)
        first = jnp.min(jnp.where(logits == mx, eid, N_EXPERTS), axis=0, keepdims=True)
        topi_ref[k:k + 1, :] = first
        vals.append(mx)
        logits = jnp.where(eid == first, NEG_INF, logits)
    es = [jnp.exp(v - vals[0]) for v in vals]
    den = es[0] + es[1] + es[2] + es[3]
    for k in range(TOP_K):
        topg_ref[k:k + 1, :] = es[k] / den


def _mixout(x2, c2, a2, gates, w_conv_out, w_attn_out, w_out, norm2_g, w_router, b_router, tm):
    t, d = x2.shape
    wr_t = w_router.T
    wrh = wr_t.astype(BF16)
    wrl = (wr_t - wrh.astype(F32)).astype(BF16)
    args = (x2, c2, a2, gates, w_conv_out.astype(BF16), w_attn_out.astype(BF16), w_out.astype(BF16),
            norm2_g.reshape(1, d), jnp.concatenate([wrh, wrl], axis=0), b_router.reshape(N_EXPERTS, 1))
    full = lambda a: pl.BlockSpec(a.shape, lambda i: (0,) * a.ndim)
    row = lambda a: pl.BlockSpec((tm, a.shape[1]), lambda i: (i, 0))
    return pl.pallas_call(
        _mixout_kernel,
        grid=(t // tm,),
        in_specs=[row(a) for a in args[:4]] + [full(a) for a in args[4:]],
        out_specs=[pl.BlockSpec((tm, d), lambda i: (i, 0)), pl.BlockSpec((tm * ROW_CHUNKS, 128), lambda i: (i, 0)),
                   pl.BlockSpec((TOP_K, tm), lambda i: (0, i)), pl.BlockSpec((TOP_K, tm), lambda i: (0, i))],
        out_shape=[jax.ShapeDtypeStruct((t, d), F32), jax.ShapeDtypeStruct((t * ROW_CHUNKS, 128), U32),
                   jax.ShapeDtypeStruct((TOP_K, t), I32), jax.ShapeDtypeStruct((TOP_K, t), F32)],
        compiler_params=_cparams(("arbitrary",)),
        name="mixout",
    )(*args)


def _rank_kernel(topi_ref, tri_ref, rank_ref, cnt_ref, run_ref):
    i = pl.program_id(0)
    tr = topi_ref.shape[1]

    @pl.when(i == 0)
    def _():
        run_ref[...] = jnp.zeros_like(run_ref)

    eid = lax.broadcasted_iota(I32, (N_EXPERTS, tr), 0)
    onehots = [(eid == topi_ref[k:k + 1, :]).astype(F32) for k in range(TOP_K)]
    tot = onehots[0] + onehots[1] + onehots[2] + onehots[3]
    before = _dot(tot.astype(BF16), tri_ref[...])
    base = run_ref[...] + before
    for k in range(TOP_K):
        rank_ref[k:k + 1, :] = jnp.sum(onehots[k] * base, axis=0, keepdims=True).astype(I32)
    run_ref[...] = run_ref[...] + jnp.sum(tot, axis=1, keepdims=True)
    cnt_ref[...] = run_ref[...]


def _rank(topi, tr):
    t = topi.shape[1]
    tri = (jnp.arange(tr)[:, None] < jnp.arange(tr)[None, :]).astype(BF16)
    return pl.pallas_call(
        _rank_kernel,
        grid=(t // tr,),
        in_specs=[pl.BlockSpec((TOP_K, tr), lambda i: (0, i)), pl.BlockSpec((tr, tr), lambda i: (0, 0))],
        out_specs=[pl.BlockSpec((TOP_K, tr), lambda i: (0, i)), pl.BlockSpec((N_EXPERTS, 1), lambda i: (0, 0))],
        out_shape=[jax.ShapeDtypeStruct((TOP_K, t), I32), jax.ShapeDtypeStruct((N_EXPERTS, 1), F32)],
        scratch_shapes=[pltpu.VMEM((N_EXPERTS, 1), F32)],
        compiler_params=_cparams(("arbitrary",)),
        name="rank",
    )(topi, tri)


def _dest_kernel(n_blocks, topi_ref, rank_ref, cnt_ref, dest_ref, blke_ref, meta_ref):
    td = topi_ref.shape[1]
    cnt = cnt_ref[...]
    padded = jnp.floor((cnt + (EXPERT_ROWS - 1)) * (1.0 / EXPERT_ROWS)) * EXPERT_ROWS
    er = lax.broadcasted_iota(I32, (N_EXPERTS, N_EXPERTS), 0)
    ec = lax.broadcasted_iota(I32, (N_EXPERTS, N_EXPERTS), 1)
    pad_row = jnp.sum(jnp.where(er == ec, padded, 0.0), axis=0, keepdims=True)
    start = jnp.sum(jnp.where(ec < er, pad_row, 0.0), axis=1, keepdims=True)
    end = start + padded
    eid = lax.broadcasted_iota(I32, (N_EXPERTS, td), 0)
    for k in range(TOP_K):
        oh = eid == topi_ref[k:k + 1, :]
        s_k = jnp.sum(jnp.where(oh, start, 0.0), axis=0, keepdims=True)
        dest_ref[0, k:k + 1, :] = (s_k.astype(I32) + rank_ref[k:k + 1, :]) * ROW_CHUNKS
    bstart = (lax.broadcasted_iota(I32, (N_EXPERTS, n_blocks), 1) * EXPERT_ROWS).astype(F32)
    be = jnp.sum((end <= bstart).astype(I32), axis=0, keepdims=True)
    blke_ref[...] = jnp.minimum(be, N_EXPERTS - 1)
    er2 = lax.broadcasted_iota(I32, (N_EXPERTS, 128), 0)
    lane = lax.broadcasted_iota(I32, (N_EXPERTS, 128), 1)
    last = jnp.maximum(end * (1.0 / EXPERT_ROWS) - 1.0, 0.0)
    used = jnp.where(er2 == N_EXPERTS - 1, end * (1.0 / EXPERT_ROWS), 0.0)
    meta = jnp.where(lane == er2, last, 0.0) + jnp.where(lane == N_EXPERTS, used, 0.0)
    meta_ref[...] = jnp.sum(meta, axis=0, keepdims=True).astype(I32)


def _dest(topi, rank, cnt, td, n_blocks):
    t = topi.shape[1]
    return pl.pallas_call(
        functools.partial(_dest_kernel, n_blocks),
        grid=(t // td,),
        in_specs=[pl.BlockSpec((TOP_K, td), lambda i: (0, i)), pl.BlockSpec((TOP_K, td), lambda i: (0, i)),
                  pl.BlockSpec((N_EXPERTS, 1), lambda i: (0, 0))],
        out_specs=[pl.BlockSpec((1, TOP_K, td), lambda i: (i, 0, 0)),
                   pl.BlockSpec((1, n_blocks), lambda i: (0, 0)), pl.BlockSpec((1, 128), lambda i: (0, 0))],
        out_shape=[jax.ShapeDtypeStruct((t // td, TOP_K, td), I32), jax.ShapeDtypeStruct((1, n_blocks), I32),
                   jax.ShapeDtypeStruct((1, 128), I32)],
        compiler_params=_cparams(("arbitrary",)),
        name="dest",
    )(topi, rank, cnt)


def _row_dma_start(n_tok, row_copy):
    def issue(it, carry):
        for u in range(DMA_UNROLL):
            for k in range(TOP_K):
                row_copy(it * DMA_UNROLL + u, k).start(priority=k % 2)
        return carry

    lax.fori_loop(0, n_tok // DMA_UNROLL, issue, 0)


def _row_dma_wait(n_tok, row_copy):
    def drain(it, carry):
        for u in range(DMA_UNROLL):
            for k in range(TOP_K):
                row_copy(it * DMA_UNROLL + u, k).wait()
        return carry

    lax.fori_loop(0, n_tok // DMA_UNROLL, drain, 0)


def _dest_tables(dest_hbm, tables, sem_idx, n_tiles, use):
    i = pl.program_id(0)
    fetch = lambda tile, slot: pltpu.make_async_copy(dest_hbm.at[tile], tables[slot], sem_idx.at[slot])

    @pl.when(i == 0)
    def _():
        fetch(0, 0).start()

    for slot in range(2):
        @pl.when((i % 2 == slot) & (i < n_tiles))
        def _():
            fetch(i, slot).wait()

            @pl.when(i + 1 < n_tiles)
            def _():
                fetch(i + 1, 1 - slot).start()

            use(slot)


def _dispatch_kernel(dest_hbm, u2_ref, xs_in, xs_hbm, dest_a, dest_b, sem_idx, sem):
    del xs_in
    td = u2_ref.shape[0] // ROW_CHUNKS
    tables = (dest_a, dest_b)

    def scatter(slot):
        def row_copy(t, k):
            src = pl.multiple_of(t * ROW_CHUNKS, ROW_CHUNKS)
            dst = pl.multiple_of(tables[slot][k * td + t], ROW_CHUNKS)
            return pltpu.make_async_copy(u2_ref.at[pl.ds(src, ROW_CHUNKS)], xs_hbm.at[pl.ds(dst, ROW_CHUNKS)], sem)

        _row_dma_start(td, row_copy)
        _row_dma_wait(td, row_copy)

    _dest_tables(dest_hbm, tables, sem_idx, pl.num_programs(0), scatter)


def _zfill_kernel(meta_ref, o_ref):
    del meta_ref
    o_ref[...] = jnp.zeros(o_ref.shape, U32)


def _dispatch(dest, u2p, meta, n_rows):
    n_tiles, per_tile = dest.shape
    td = per_tile // TOP_K
    blk_rows = EXPERT_ROWS * ROW_CHUNKS
    xs0 = pl.pallas_call(
        _zfill_kernel,
        grid_spec=pltpu.PrefetchScalarGridSpec(
            num_scalar_prefetch=1, grid=(N_EXPERTS,), in_specs=[],
            out_specs=pl.BlockSpec((blk_rows, 128), lambda e, mt: (mt[e], 0))),
        out_shape=jax.ShapeDtypeStruct((n_rows * ROW_CHUNKS, 128), U32),
        compiler_params=_cparams(("arbitrary",)),
        name="zfill",
    )(meta)
    return pl.pallas_call(
        _dispatch_kernel,
        grid=(n_tiles,),
        in_specs=[pl.BlockSpec(memory_space=pl.ANY), pl.BlockSpec((td * ROW_CHUNKS, 128), lambda i: (i, 0)),
                  pl.BlockSpec(memory_space=pl.ANY)],
        out_specs=pl.BlockSpec(memory_space=pl.ANY),
        out_shape=jax.ShapeDtypeStruct(xs0.shape, U32),
        scratch_shapes=[pltpu.SMEM((per_tile,), I32), pltpu.SMEM((per_tile,), I32),
                        pltpu.SemaphoreType.DMA((2,)), pltpu.SemaphoreType.DMA],
        input_output_aliases={2: 0},
        compiler_params=_cparams(("arbitrary",)),
        name="dispatch",
    )(dest, u2p, xs0)


def _used_block(i, meta_ref):
    return jnp.minimum(i, meta_ref[N_EXPERTS] - 1)


def _expert_kernel(blke_ref, meta_ref, xs_ref, wug_ref, bug_ref, wd_ref, bd_ref, ys_ref, wug_bf, wd_bf):
    i = pl.program_id(0)
    f = wd_ref.shape[1]
    cur = blke_ref[_used_block(i, meta_ref)]
    prev = blke_ref[_used_block(jnp.maximum(i - 1, 0), meta_ref)]

    @pl.when((i == 0) | (cur != prev))
    def _():
        wug_bf[...] = wug_ref[0].astype(BF16)
        wd_bf[...] = wd_ref[0].astype(BF16)

    @pl.when(i < meta_ref[N_EXPERTS])
    def _():
        x = _load_rows(xs_ref, EXPERT_ROWS).astype(BF16)
        h = _dot(x, wug_bf[...]) + bug_ref[0]
        a = jnp.minimum(h[:, :f], SWIGLU_LIMIT)
        lin = jnp.clip(h[:, f:], -SWIGLU_LIMIT, SWIGLU_LIMIT)
        act = (a * _sigmoid(SWIGLU_ALPHA * a) * (lin + 1.0)).astype(BF16)
        y = _dot(act, wd_bf[...]) + bd_ref[0]
        _store_rows(ys_ref, y)


def _experts(blke, meta, xs, w_up_gate, b_up_gate, w_down, b_down):
    e, d, f2 = w_up_gate.shape
    blk_rows = EXPERT_ROWS * ROW_CHUNKS
    n_blocks = xs.shape[0] // blk_rows
    row_blk = lambda i, be, mt: (_used_block(i, mt), 0)
    expert = lambda i, be, mt: (be[_used_block(i, mt)], 0, 0)
    return pl.pallas_call(
        _expert_kernel,
        grid_spec=pltpu.PrefetchScalarGridSpec(
            num_scalar_prefetch=2,
            grid=(n_blocks,),
            in_specs=[
                pl.BlockSpec((blk_rows, 128), row_blk),
                pl.BlockSpec((1, d, f2), expert),
                pl.BlockSpec((1, 1, f2), expert),
                pl.BlockSpec((1, f2 // 2, d), expert),
                pl.BlockSpec((1, 1, d), expert),
            ],
            out_specs=pl.BlockSpec((blk_rows, 128), row_blk),
            scratch_shapes=[pltpu.VMEM((d, f2), BF16), pltpu.VMEM((f2 // 2, d), BF16)],
        ),
        out_shape=jax.ShapeDtypeStruct(xs.shape, U32),
        compiler_params=_cparams(("arbitrary",)),
        name="experts",
    )(blke, meta, xs, w_up_gate, b_up_gate.reshape(e, 1, f2), w_down, b_down.reshape(e, 1, d))


def _combine_kernel(dest_hbm, ys_hbm, h1_ref, topg_ref, o_ref, *scratch):
    i = pl.program_id(0)
    n_tiles = pl.num_programs(0) - 1
    tc = h1_ref.shape[0]
    dest_smem = scratch[0:2]
    bufs = (scratch[2:2 + TOP_K], scratch[2 + TOP_K:2 + 2 * TOP_K])
    sem_idx, sems = scratch[2 + 2 * TOP_K], scratch[3 + 2 * TOP_K]

    def row_copy(slot, indexed):
        def make(t, k):
            src = pl.multiple_of(dest_smem[slot][k * tc + t], ROW_CHUNKS) if indexed else 0
            dst = pl.multiple_of(t * ROW_CHUNKS, ROW_CHUNKS)
            return pltpu.make_async_copy(ys_hbm.at[pl.ds(src, ROW_CHUNKS)],
                                         bufs[slot][k].at[pl.ds(dst, ROW_CHUNKS)], sems.at[slot])
        return make

    def start(slot):
        _row_dma_start(tc, row_copy(slot, True))

    def finish(slot):
        _row_dma_wait(tc, row_copy(slot, False))
        g = jnp.concatenate([topg_ref[...], jnp.zeros((8 - TOP_K, tc), F32)], axis=0).T
        out = h1_ref[...]
        for k in range(TOP_K):
            out = out + g[:, k:k + 1] * _load_rows(bufs[slot][k], tc)
        o_ref[...] = out

    _dest_tables(dest_hbm, dest_smem, sem_idx, n_tiles, start)

    for slot in range(2):
        @pl.when((i % 2 != slot) & (i > 0))
        def _():
            finish(slot)


def _combine(dest, ys, h1, topg):
    n_tiles, per_tile = dest.shape
    tc = per_tile // TOP_K
    t, d = h1.shape
    prev = lambda i: jnp.maximum(i - 1, 0)
    return pl.pallas_call(
        _combine_kernel,
        grid=(n_tiles + 1,),
        in_specs=[pl.BlockSpec(memory_space=pl.ANY), pl.BlockSpec(memory_space=pl.ANY),
                  pl.BlockSpec((tc, d), lambda i: (prev(i), 0)), pl.BlockSpec((TOP_K, tc), lambda i: (0, prev(i)))],
        out_specs=pl.BlockSpec((tc, d), lambda i: (prev(i), 0)),
        out_shape=jax.ShapeDtypeStruct((t, d), F32),
        scratch_shapes=[pltpu.SMEM((per_tile,), I32)] * 2 + [pltpu.VMEM((tc * ROW_CHUNKS, 128), U32)] * (2 * TOP_K)
                       + [pltpu.SemaphoreType.DMA((2,)), pltpu.SemaphoreType.DMA((2,))],
        compiler_params=_cparams(("arbitrary",)),
        name="combine",
    )(dest, ys, h1, topg)


def _tile(n, pref):
    return pref if n % pref == 0 else n


def _layer(h, norm1_g, w_in, b_gates, conv_dw, conv_dw_b, conv_ln_g, conv_ln_b, w_conv_out,
           q_norm_g, k_norm_g, w_attn_out, w_out, norm2_g, w_router, b_router,
           w_up_gate, b_up_gate, w_down, b_down):
    b, s, d = h.shape
    t = b * s
    assert s % MOBA_BLOCK == 0 and s // MOBA_BLOCK <= GATE_SLOTS
    tm = _tile(s, 512)
    x2 = h.reshape(t, d)
    hglu, q, k, v, gates, gt = _inproj(x2, norm1_g, w_in, b_gates, q_norm_g, k_norm_g, s, tm)
    c = hglu.shape[1]
    aw = q.shape[1]
    conv = _conv(hglu.reshape(b, s, c), conv_dw, conv_dw_b, conv_ln_g, conv_ln_b, tm)
    attn = _attention(q.reshape(b, s, aw), k.reshape(b, s, N_HEADS * K_AUG), v.reshape(b, s, aw), gt, tm)
    h1, u2p, topi, topg = _mixout(x2, conv.reshape(t, c), attn.reshape(t, aw), gates, w_conv_out, w_attn_out,
                                  w_out, norm2_g, w_router, b_router, tm)
    n_blocks = -(-(t * TOP_K) // EXPERT_ROWS) + N_EXPERTS
    rank, cnt = _rank(topi, _tile(t, 1024))
    dest, blke, meta = _dest(topi, rank, cnt, tm, n_blocks)
    dest = dest.reshape(t // tm, TOP_K * tm)
    meta = meta.reshape(128)
    xs = _dispatch(dest, u2p, meta, n_blocks * EXPERT_ROWS)
    ys = _experts(blke.reshape(n_blocks), meta, xs, w_up_gate, b_up_gate, w_down, b_down)
    out = _combine(dest, ys, h1, topg)
    return out.reshape(b, s, d)


def kernel(x, norm1_g, w_in, b_gates, conv_dw, conv_dw_b, conv_ln_g, conv_ln_b, w_conv_out, q_norm_g, k_norm_g,
           w_attn_out, w_out, norm2_g, w_router, b_router, w_up_gate, b_up_gate, w_down, b_down):
    h = x
    for l in range(norm1_g.shape[0]):
        h = _layer(h, norm1_g[l], w_in[l], b_gates[l], conv_dw[l], conv_dw_b[l], conv_ln_g[l], conv_ln_b[l],
                   w_conv_out[l], q_norm_g[l], k_norm_g[l], w_attn_out[l], w_out[l], norm2_g[l], w_router[l],
                   b_router[l], w_up_gate[l], b_up_gate[l], w_down[l], b_down[l])
    return h
```

```python
import functools

import jax
import jax.numpy as jnp
from jax import lax
from jax.experimental import pallas as pl
from jax.experimental.pallas import tpu as pltpu

F32 = jnp.float32
BF16 = jnp.bfloat16
I32 = jnp.int32
U32 = jnp.uint32

EPS = 1e-6
CONV_WIDTH = 31
HALO = 32
N_HEADS = 8
HEAD_DIM = 64
MOBA_BLOCK = 256
MOBA_TOPK = 3
GATE_SLOTS = 16
HEADS_PER_GROUP = 4
K_AUG = 128
AUG_POS = 16
V_ROWS = 80
MASKED = -1e30
LOG2E = 1.4426950408889634
M_INIT = -3e38
ATTN_UNROLL = 4
ATTN_LOOKAHEAD = 6
ATTN_SLOTS = 8
N_EXPERTS = 32
TOP_K = 4
SWIGLU_LIMIT = 7.0
SWIGLU_ALPHA = 1.702
EXPERT_ROWS = 512
ROW_CHUNKS = 4
DMA_UNROLL = 8
NEG_INF = float("-inf")
VMEM_LIMIT = 56 * 1024 * 1024


def _cparams(sem):
    return pltpu.CompilerParams(dimension_semantics=sem, vmem_limit_bytes=VMEM_LIMIT)


def _dot(a, b):
    return jnp.dot(a, b, preferred_element_type=F32)


def _dot_nt(a, b):
    return lax.dot_general(a, b, (((1,), (1,)), ((), ())), preferred_element_type=F32)


def _split_bf16(x):
    hi = x.astype(BF16)
    lo = (x - hi.astype(F32)).astype(BF16)
    return hi, lo


def _sigmoid(x):
    return 1.0 / (1.0 + jnp.exp(-x))


def _inproj_kernel(tiles_per_seq, x_ref, g1_ref, wglu_ref, wq_ref, wk_ref, wv_ref, wg_ref, bg_ref,
                   qg_ref, kg_ref, bd_ref, place_ref,
                   h_ref, q_ref, k_ref, v_ref, gate_ref, gt_ref, km_ref):
    i = pl.program_id(0)
    tm = x_ref.shape[0]
    c = h_ref.shape[1]
    aw = q_ref.shape[1]

    @pl.when(i == 0)
    def _():
        km_ref[...] = jnp.zeros_like(km_ref)

    x = x_ref[...]
    ms = jnp.mean(x * x, axis=-1, keepdims=True)
    u = (x * lax.rsqrt(ms + EPS) * g1_ref[...]).astype(BF16)

    glu = _dot(u, wglu_ref[...])
    h_ref[...] = (glu[:, :c] * _sigmoid(glu[:, c:])).astype(BF16)

    bd = bd_ref[...]

    def head_norm(t, g):
        ss = _dot((t * t).astype(BF16), bd)
        return t * lax.rsqrt(ss * (1.0 / HEAD_DIM) + EPS) * g

    qn = head_norm(_dot(u, wq_ref[...]), qg_ref[...]) * (HEAD_DIM ** -0.5 * LOG2E)
    kn = head_norm(_dot(u, wk_ref[...]), kg_ref[...])
    q_hi, q_lo = _split_bf16(qn)
    q_ref[...] = q_hi
    pos = lax.broadcasted_iota(I32, (tm, K_AUG), 0) + (i % tiles_per_seq) * tm
    col = lax.broadcasted_iota(I32, (tm, K_AUG), 1)
    kpos = pos & (MOBA_BLOCK - 1)
    bstart = pos - kpos
    aug = jnp.where((col >= HEAD_DIM) & (col < HEAD_DIM + 3), kpos,
                    jnp.where((col >= HEAD_DIM + 3) & (col < HEAD_DIM + 6), bstart,
                              jnp.where((col - (HEAD_DIM + AUG_POS)) * MOBA_BLOCK == bstart, 1, 0))).astype(F32)
    kp = _dot(kn.astype(BF16), place_ref[...])
    for h in range(N_HEADS):
        k_ref[:, h * K_AUG:(h + 1) * K_AUG] = (kp[:, h * K_AUG:(h + 1) * K_AUG] + aug).astype(BF16)
    v_ref[...] = _dot(u, wv_ref[...]).astype(BF16)
    gate_ref[...] = _sigmoid(_dot(u, wg_ref[...]) + bg_ref[...]).astype(BF16)

    lane_head = lax.broadcasted_iota(I32, (1, aw), 1) // HEAD_DIM
    blocks_per_tile = tm // MOBA_BLOCK
    n0 = (i % tiles_per_seq) * blocks_per_tile
    for j in range(blocks_per_tile):
        kmj = jnp.mean(kn[j * MOBA_BLOCK:(j + 1) * MOBA_BLOCK], axis=0, keepdims=True)
        for h in range(N_HEADS):
            km_ref[pl.ds(h * GATE_SLOTS + n0 + j, 1), :] = jnp.where(lane_head == h, kmj, 0.0)

    km_hi, km_lo = _split_bf16(km_ref[...])
    gt_ref[0] = _dot_nt(km_hi, q_hi) + _dot_nt(km_lo, q_hi) + _dot_nt(km_hi, q_lo)


def _inproj(x2, norm1_g, w_in, b_gates, q_norm_g, k_norm_g, seq, tm):
    t, d = x2.shape
    aw = N_HEADS * HEAD_DIM
    c = (w_in.shape[1] - 3 * aw - 2 * d) // 2
    c1, c2, c3, c4 = 2 * c, 2 * c + aw, 2 * c + 2 * aw, 2 * c + 3 * aw
    wb = w_in.astype(BF16)
    hd = jnp.arange(aw, dtype=I32) // HEAD_DIM
    bd = (hd[:, None] == hd[None, :]).astype(BF16)
    lane = jnp.arange(aw, dtype=I32)
    place = ((lane // HEAD_DIM * K_AUG + lane % HEAD_DIM)[:, None]
             == jnp.arange(N_HEADS * K_AUG, dtype=I32)[None, :]).astype(BF16)
    n_tiles = t // tm
    full = lambda a: pl.BlockSpec(a.shape, lambda i: (0,) * a.ndim)
    row = lambda n: pl.BlockSpec((tm, n), lambda i: (i, 0))
    args = (x2, norm1_g.reshape(1, d), wb[:, :c1], wb[:, c1:c2], wb[:, c2:c3], wb[:, c3:c4], wb[:, c4:],
            b_gates.reshape(1, 2 * d), jnp.tile(q_norm_g, N_HEADS).reshape(1, aw),
            jnp.tile(k_norm_g, N_HEADS).reshape(1, aw), bd, place)
    return pl.pallas_call(
        functools.partial(_inproj_kernel, seq // tm),
        grid=(n_tiles,),
        in_specs=[row(d)] + [full(a) for a in args[1:]],
        out_specs=[row(c), row(aw), row(N_HEADS * K_AUG), row(aw), row(2 * d),
                   pl.BlockSpec((1, N_HEADS * GATE_SLOTS, tm), lambda i: (i, 0, 0))],
        out_shape=[jax.ShapeDtypeStruct((t, c), BF16), jax.ShapeDtypeStruct((t, aw), BF16),
                   jax.ShapeDtypeStruct((t, N_HEADS * K_AUG), BF16), jax.ShapeDtypeStruct((t, aw), BF16),
                   jax.ShapeDtypeStruct((t, 2 * d), BF16),
                   jax.ShapeDtypeStruct((n_tiles, N_HEADS * GATE_SLOTS, tm), F32)],
        scratch_shapes=[pltpu.VMEM((N_HEADS * GATE_SLOTS, aw), F32)],
        compiler_params=_cparams(("arbitrary",)),
        name="inproj",
    )(*args)


def _conv_kernel(h_ref, dw_ref, dwb_ref, lng_ref, lnb_ref, o_ref, win_ref, sh_ref):
    s = pl.program_id(1)
    ts = h_ref.shape[1]

    @pl.when(s == 0)
    def _():
        win_ref[0:HALO, :] = jnp.zeros((HALO, win_ref.shape[1]), F32)

    @pl.when(s != 0)
    def _():
        win_ref[0:HALO, :] = win_ref[ts:ts + HALO, :]

    win_ref[HALO:HALO + ts, :] = h_ref[0].astype(F32)
    span = ts + HALO - 8
    for r in range(1, 8):
        sh_ref[r - 1] = win_ref[r:r + span, :]
    acc = jnp.zeros((ts, win_ref.shape[1]), F32) + dwb_ref[...]
    for w in range(CONV_WIDTH):
        off = HALO - (CONV_WIDTH - 1) + w
        r = off % 8
        tap = win_ref[off:off + ts, :] if r == 0 else sh_ref[r - 1, off - r:off - r + ts, :]
        acc = acc + dw_ref[w:w + 1, :] * tap
    mu = jnp.mean(acc, axis=-1, keepdims=True)
    cen = acc - mu
    var = jnp.mean(cen * cen, axis=-1, keepdims=True)
    y = cen * lax.rsqrt(var + EPS) * lng_ref[...] + lnb_ref[...]
    o_ref[0] = (y * _sigmoid(y)).astype(BF16)


def _conv(h3, dw, dwb, lng, lnb, ts):
    b, s, c = h3.shape
    vec = lambda a: pl.BlockSpec(a.shape, lambda i, j: (0, 0))
    args = (h3, dw, dwb.reshape(1, c), lng.reshape(1, c), lnb.reshape(1, c))
    return pl.pallas_call(
        _conv_kernel,
        grid=(b, s // ts),
        in_specs=[pl.BlockSpec((1, ts, c), lambda i, j: (i, j, 0))] + [vec(a) for a in args[1:]],
        out_specs=pl.BlockSpec((1, ts, c), lambda i, j: (i, j, 0)),
        out_shape=jax.ShapeDtypeStruct((b, s, c), BF16),
        scratch_shapes=[pltpu.VMEM((ts + HALO, c), F32), pltpu.VMEM((7, ts + HALO - 8, c), F32)],
        compiler_params=_cparams(("arbitrary", "arbitrary")),
        name="conv",
    )(*args)


def _attn_kernel(q_ref, k_ref, v_ref, gt_ref, srow_ref, o_ref,
                 vt_ref, caus_ref, w_ref, s_ref, m_ref, acc_ref):
    qb = pl.program_id(2)
    blk = MOBA_BLOCK
    hp = HEADS_PER_GROUP
    n_blocks = v_ref.shape[1] // blk

    @pl.when(qb == 0)
    def _():
        ones_rows = (lax.broadcasted_iota(I32, (V_ROWS - HEAD_DIM, blk), 0) == 0).astype(BF16)
        for n in range(n_blocks):
            vt = v_ref[0, n * blk:(n + 1) * blk, :].astype(F32).T
            for h in range(hp):
                vt_ref[n, h, 0:HEAD_DIM, :] = vt[h * HEAD_DIM:(h + 1) * HEAD_DIM].astype(BF16)
                vt_ref[n, h, HEAD_DIM:V_ROWS, :] = ones_rows
        kpos = lax.broadcasted_iota(I32, (blk, blk), 0)
        qpos = lax.broadcasted_iota(I32, (blk, blk), 1)
        caus_ref[0] = jnp.zeros((blk, blk), F32)
        caus_ref[1] = jnp.where(kpos <= qpos, 0.0, NEG_INF)
        for h in range(hp):
            w_ref[h, HEAD_DIM:HEAD_DIM + AUG_POS, :] = srow_ref[h]
            w_ref[h, HEAD_DIM + AUG_POS + GATE_SLOTS:K_AUG, :] = jnp.zeros(
                (K_AUG - HEAD_DIM - AUG_POS - GATE_SLOTS, blk), BF16)

    qt = q_ref[0].astype(F32).T
    slot = lax.broadcasted_iota(I32, (GATE_SLOTS, blk), 0)
    for h in range(hp):
        g = gt_ref[0, h * GATE_SLOTS:(h + 1) * GATE_SLOTS, :]
        g = jnp.where(slot < qb, g, NEG_INF)
        sel = slot == qb
        for _ in range(MOBA_TOPK):
            mx = jnp.max(g, axis=0, keepdims=True)
            first = jnp.min(jnp.where(g == mx, slot, GATE_SLOTS), axis=0, keepdims=True)
            pick = (slot == first) & (mx > NEG_INF)
            sel = sel | pick
            g = jnp.where(pick, NEG_INF, g)
        w_ref[h, 0:HEAD_DIM, :] = qt[h * HEAD_DIM:(h + 1) * HEAD_DIM].astype(BF16)
        w_ref[h, HEAD_DIM + AUG_POS:HEAD_DIM + AUG_POS + GATE_SLOTS, :] = jnp.where(sel, 0.0, MASKED).astype(BF16)

    m_ref[...] = jnp.full(m_ref.shape, M_INIT, F32)
    acc_ref[...] = jnp.zeros(acc_ref.shape, F32)

    def scores(n, h, slot):
        start = pl.multiple_of(n * blk, blk)
        t = _dot(k_ref[0, pl.ds(start, blk), h * K_AUG:(h + 1) * K_AUG], w_ref[h])
        s_ref[slot] = t + caus_ref[jnp.where(n == qb, 1, 0)]

    def accumulate(n, h, slot):
        t = s_ref[slot]
        m_old = m_ref[h]
        m_new = jnp.maximum(m_old, jnp.max(t, axis=0, keepdims=True))
        p = jnp.exp2(t - m_new).astype(BF16)
        pv = _dot(vt_ref[n, h], p)
        m_ref[h] = m_new
        acc_ref[h] = jnp.exp2(m_old - m_new) * acc_ref[h] + pv

    def run_blocks(first, count):
        units = [(first + u, h) for u in range(count) for h in range(hp)]
        ahead = min(ATTN_LOOKAHEAD, len(units))
        for j in range(ahead):
            scores(*units[j], j % ATTN_SLOTS)
        for j, (n, h) in enumerate(units):
            if j + ahead < len(units):
                scores(*units[j + ahead], (j + ahead) % ATTN_SLOTS)
            accumulate(n, h, j % ATTN_SLOTS)

    def chunk(c, carry):
        run_blocks(c * ATTN_UNROLL, ATTN_UNROLL)
        return carry

    full = (qb + 1) // ATTN_UNROLL
    lax.fori_loop(0, full, chunk, 0)
    for left in range(1, ATTN_UNROLL):
        @pl.when(qb + 1 - full * ATTN_UNROLL == left)
        def _():
            run_blocks(full * ATTN_UNROLL, left)
    outs = [acc_ref[h, 0:HEAD_DIM, :] / acc_ref[h, HEAD_DIM:HEAD_DIM + 1, :] for h in range(hp)]
    o_ref[0] = jnp.concatenate(outs, axis=0).T.astype(BF16)


def _slope_rows():
    slopes = 2.0 ** (-8.0 * jnp.arange(1, N_HEADS + 1, dtype=F32) / N_HEADS) * LOG2E
    s1 = slopes.astype(BF16)
    r1 = slopes - s1.astype(F32)
    s2 = r1.astype(BF16)
    s3 = (r1 - s2.astype(F32)).astype(BF16)
    parts = jnp.stack([s1, s2, s3, s1, s2, s3] + [jnp.zeros_like(s1)] * (AUG_POS - 6), axis=1)
    return jnp.broadcast_to(parts[:, :, None], (N_HEADS, AUG_POS, MOBA_BLOCK))


def _attention(q3, kaug3, v3, gt, tm):
    b, s, aw = q3.shape
    blk = MOBA_BLOCK
    hp = HEADS_PER_GROUP
    gw = hp * HEAD_DIM
    n_groups = aw // gw
    nb = s // blk
    assert nb <= GATE_SLOTS
    per_tile = tm // blk
    tiles_per_seq = s // tm
    return pl.pallas_call(
        _attn_kernel,
        grid=(b, n_groups, nb),
        in_specs=[
            pl.BlockSpec((1, blk, gw), lambda i, g, j: (i, j, g)),
            pl.BlockSpec((1, s, hp * K_AUG), lambda i, g, j: (i, 0, g)),
            pl.BlockSpec((1, s, gw), lambda i, g, j: (i, 0, g)),
            pl.BlockSpec((1, hp * GATE_SLOTS, blk),
                         lambda i, g, j: (i * tiles_per_seq + j // per_tile, g, j % per_tile)),
            pl.BlockSpec((hp, AUG_POS, blk), lambda i, g, j: (g, 0, 0)),
        ],
        out_specs=pl.BlockSpec((1, blk, gw), lambda i, g, j: (i, j, g)),
        scratch_shapes=[
            pltpu.VMEM((nb, hp, V_ROWS, blk), BF16),
            pltpu.VMEM((2, blk, blk), F32),
            pltpu.VMEM((hp, K_AUG, blk), BF16),
            pltpu.VMEM((ATTN_SLOTS, blk, blk), F32),
            pltpu.VMEM((hp, 1, blk), F32),
            pltpu.VMEM((hp, V_ROWS, blk), F32),
        ],
        out_shape=jax.ShapeDtypeStruct((b, s, aw), BF16),
        compiler_params=_cparams(("arbitrary", "arbitrary", "arbitrary")),
        name="attn",
    )(q3, kaug3, v3, gt, _slope_rows())


def _pack_bf16_pairs(x):
    n = x.shape[1] // 2
    r = x.astype(BF16).astype(F32)
    lo = lax.shift_right_logical(pltpu.bitcast(r[:, :n], U32), jnp.uint32(16))
    hi = pltpu.bitcast(r[:, n:], U32) & jnp.uint32(0xFFFF0000)
    return hi | lo


def _store_rows(ref, x):
    pk = _pack_bf16_pairs(x)
    m, n = pk.shape
    c = n // 128
    for j in range(c):
        ref[pl.ds(j, m, stride=c), :] = pk[:, j * 128:(j + 1) * 128]


def _load_rows(ref, m):
    c = ref.shape[0] // m
    ws = [ref[pl.ds(j, m, stride=c), :] for j in range(c)]
    lo = [pltpu.bitcast(lax.shift_left(w, jnp.uint32(16)), F32) for w in ws]
    hi = [pltpu.bitcast(w & jnp.uint32(0xFFFF0000), F32) for w in ws]
    return jnp.concatenate(lo + hi, axis=1)


def _mixout_kernel(x_ref, c_ref, a_ref, gate_ref, wco_ref, wao_ref, wout_ref, g2_ref,
                   wr_ref, br_ref,
                   h1_ref, u2_ref, topi_ref, topg_ref):
    d = x_ref.shape[1]
    y_c = _dot(c_ref[...], wco_ref[...])
    y_a = _dot(a_ref[...], wao_ref[...])
    gate = gate_ref[...].astype(F32)
    mixed = (gate[:, :d] * y_c + gate[:, d:] * y_a).astype(BF16)
    h1 = x_ref[...] + _dot(mixed, wout_ref[...])
    h1_ref[...] = h1
    ms = jnp.mean(h1 * h1, axis=-1, keepdims=True)
    u2 = h1 * lax.rsqrt(ms + EPS) * g2_ref[...]
    _store_rows(u2_ref, u2)
    u_hi, u_lo = _split_bf16(u2)
    wr = wr_ref[...]
    both = _dot_nt(wr, u_hi)
    logits = both[:N_EXPERTS] + both[N_EXPERTS:] + _dot_nt(wr[:N_EXPERTS], u_lo) + br_ref[...]
    eid = lax.broadcasted_iota(I32, logits.shape, 0)
    vals = []
    for k in range(TOP_K):
        mx = jnp.max(logits, axis=0, keepdims=True)
        first = jnp.min(jnp.where(logits == mx, eid, N_EXPERTS), axis=0, keepdims=True)
        topi_ref[k:k + 1, :] = first
        vals.append(mx)
        logits = jnp.where(eid == first, NEG_INF, logits)
    es = [jnp.exp(v - vals[0]) for v in vals]
    den = es[0] + es[1] + es[2] + es[3]
    for k in range(TOP_K):
        topg_ref[k:k + 1, :] = es[k] / den


def _mixout(x2, c2, a2, gates, w_conv_out, w_attn_out, w_out, norm2_g, w_router, b_router, tm):
    t, d = x2.shape
    wr_t = w_router.T
    wrh = wr_t.astype(BF16)
    wrl = (wr_t - wrh.astype(F32)).astype(BF16)
    args = (x2, c2, a2, gates, w_conv_out.astype(BF16), w_attn_out.astype(BF16), w_out.astype(BF16),
            norm2_g.reshape(1, d), jnp.concatenate([wrh, wrl], axis=0), b_router.reshape(N_EXPERTS, 1))
    full = lambda a: pl.BlockSpec(a.shape, lambda i: (0,) * a.ndim)
    row = lambda a: pl.BlockSpec((tm, a.shape[1]), lambda i: (i, 0))
    return pl.pallas_call(
        _mixout_kernel,
        grid=(t // tm,),
        in_specs=[row(a) for a in args[:4]] + [full(a) for a in args[4:]],
        out_specs=[pl.BlockSpec((tm, d), lambda i: (i, 0)), pl.BlockSpec((tm * ROW_CHUNKS, 128), lambda i: (i, 0)),
                   pl.BlockSpec((TOP_K, tm), lambda i: (0, i)), pl.BlockSpec((TOP_K, tm), lambda i: (0, i))],
        out_shape=[jax.ShapeDtypeStruct((t, d), F32), jax.ShapeDtypeStruct((t * ROW_CHUNKS, 128), U32),
                   jax.ShapeDtypeStruct((TOP_K, t), I32), jax.ShapeDtypeStruct((TOP_K, t), F32)],
        compiler_params=_cparams(("arbitrary",)),
        name="mixout",
    )(*args)


def _rank_kernel(topi_ref, tri_ref, rank_ref, cnt_ref, run_ref):
    i = pl.program_id(0)
    tr = topi_ref.shape[1]

    @pl.when(i == 0)
    def _():
        run_ref[...] = jnp.zeros_like(run_ref)

    eid = lax.broadcasted_iota(I32, (N_EXPERTS, tr), 0)
    onehots = [(eid == topi_ref[k:k + 1, :]).astype(F32) for k in range(TOP_K)]
    tot = onehots[0] + onehots[1] + onehots[2] + onehots[3]
    before = _dot(tot.astype(BF16), tri_ref[...])
    base = run_ref[...] + before
    for k in range(TOP_K):
        rank_ref[k:k + 1, :] = jnp.sum(onehots[k] * base, axis=0, keepdims=True).astype(I32)
    run_ref[...] = run_ref[...] + jnp.sum(tot, axis=1, keepdims=True)
    cnt_ref[...] = run_ref[...]


def _rank(topi, tr):
    t = topi.shape[1]
    tri = (jnp.arange(tr)[:, None] < jnp.arange(tr)[None, :]).astype(BF16)
    return pl.pallas_call(
        _rank_kernel,
        grid=(t // tr,),
        in_specs=[pl.BlockSpec((TOP_K, tr), lambda i: (0, i)), pl.BlockSpec((tr, tr), lambda i: (0, 0))],
        out_specs=[pl.BlockSpec((TOP_K, tr), lambda i: (0, i)), pl.BlockSpec((N_EXPERTS, 1), lambda i: (0, 0))],
        out_shape=[jax.ShapeDtypeStruct((TOP_K, t), I32), jax.ShapeDtypeStruct((N_EXPERTS, 1), F32)],
        scratch_shapes=[pltpu.VMEM((N_EXPERTS, 1), F32)],
        compiler_params=_cparams(("arbitrary",)),
        name="rank",
    )(topi, tri)


def _dest_kernel(n_blocks, topi_ref, rank_ref, cnt_ref, dest_ref, blke_ref, meta_ref):
    n_sub, _, td = dest_ref.shape
    cnt = cnt_ref[...]
    padded = jnp.floor((cnt + (EXPERT_ROWS - 1)) * (1.0 / EXPERT_ROWS)) * EXPERT_ROWS
    er = lax.broadcasted_iota(I32, (N_EXPERTS, N_EXPERTS), 0)
    ec = lax.broadcasted_iota(I32, (N_EXPERTS, N_EXPERTS), 1)
    pad_row = jnp.sum(jnp.where(er == ec, padded, 0.0), axis=0, keepdims=True)
    start = jnp.sum(jnp.where(ec < er, pad_row, 0.0), axis=1, keepdims=True)
    end = start + padded
    eid = lax.broadcasted_iota(I32, (N_EXPERTS, td), 0)
    for j in range(n_sub):
        for k in range(TOP_K):
            oh = eid == topi_ref[k:k + 1, j * td:(j + 1) * td]
            s_k = jnp.sum(jnp.where(oh, start, 0.0), axis=0, keepdims=True)
            dest_ref[j, k:k + 1, :] = (s_k.astype(I32) + rank_ref[k:k + 1, j * td:(j + 1) * td]) * ROW_CHUNKS
    bstart = (lax.broadcasted_iota(I32, (N_EXPERTS, n_blocks), 1) * EXPERT_ROWS).astype(F32)
    be = jnp.sum((end <= bstart).astype(I32), axis=0, keepdims=True)
    blke_ref[...] = jnp.minimum(be, N_EXPERTS - 1)
    er2 = lax.broadcasted_iota(I32, (N_EXPERTS, 128), 0)
    lane = lax.broadcasted_iota(I32, (N_EXPERTS, 128), 1)
    last = jnp.maximum(end * (1.0 / EXPERT_ROWS) - 1.0, 0.0)
    used = jnp.where(er2 == N_EXPERTS - 1, end * (1.0 / EXPERT_ROWS), 0.0)
    meta = jnp.where(lane == er2, last, 0.0) + jnp.where(lane == N_EXPERTS, used, 0.0)
    meta_ref[...] = jnp.sum(meta, axis=0, keepdims=True).astype(I32)


def _dest(topi, rank, cnt, td, n_blocks):
    t = topi.shape[1]
    n_sub = 4 if t % (4 * td) == 0 else 1
    wide = n_sub * td
    return pl.pallas_call(
        functools.partial(_dest_kernel, n_blocks),
        grid=(t // wide,),
        in_specs=[pl.BlockSpec((TOP_K, wide), lambda i: (0, i)), pl.BlockSpec((TOP_K, wide), lambda i: (0, i)),
                  pl.BlockSpec((N_EXPERTS, 1), lambda i: (0, 0))],
        out_specs=[pl.BlockSpec((n_sub, TOP_K, td), lambda i: (i, 0, 0)),
                   pl.BlockSpec((1, n_blocks), lambda i: (0, 0)), pl.BlockSpec((1, 128), lambda i: (0, 0))],
        out_shape=[jax.ShapeDtypeStruct((t // td, TOP_K, td), I32), jax.ShapeDtypeStruct((1, n_blocks), I32),
                   jax.ShapeDtypeStruct((1, 128), I32)],
        compiler_params=_cparams(("arbitrary",)),
        name="dest",
    )(topi, rank, cnt)


def _row_dma_start(n_tok, row_copy):
    def issue(it, carry):
        for u in range(DMA_UNROLL):
            for k in range(TOP_K):
                row_copy(it * DMA_UNROLL + u, k).start(priority=k % 2)
        return carry

    lax.fori_loop(0, n_tok // DMA_UNROLL, issue, 0)


def _row_dma_wait(n_tok, row_copy):
    def drain(it, carry):
        for u in range(DMA_UNROLL):
            for k in range(TOP_K):
                row_copy(it * DMA_UNROLL + u, k).wait()
        return carry

    lax.fori_loop(0, n_tok // DMA_UNROLL, drain, 0)


def _dest_tables(dest_hbm, tables, sem_idx, n_tiles, use):
    i = pl.program_id(0)
    fetch = lambda tile, slot: pltpu.make_async_copy(dest_hbm.at[tile], tables[slot], sem_idx.at[slot])

    @pl.when(i == 0)
    def _():
        fetch(0, 0).start()

    for slot in range(2):
        @pl.when((i % 2 == slot) & (i < n_tiles))
        def _():
            fetch(i, slot).wait()

            @pl.when(i + 1 < n_tiles)
            def _():
                fetch(i + 1, 1 - slot).start()

            use(slot)


def _dispatch_kernel(dest_hbm, u2_ref, xs_in, xs_hbm, dest_a, dest_b, sem_idx, sem):
    del xs_in
    td = u2_ref.shape[0] // ROW_CHUNKS
    tables = (dest_a, dest_b)

    def scatter(slot):
        def row_copy(t, k):
            src = pl.multiple_of(t * ROW_CHUNKS, ROW_CHUNKS)
            dst = pl.multiple_of(tables[slot][k * td + t], ROW_CHUNKS)
            return pltpu.make_async_copy(u2_ref.at[pl.ds(src, ROW_CHUNKS)], xs_hbm.at[pl.ds(dst, ROW_CHUNKS)], sem)

        _row_dma_start(td, row_copy)
        _row_dma_wait(td, row_copy)

    _dest_tables(dest_hbm, tables, sem_idx, pl.num_programs(0), scatter)


def _zfill_kernel(meta_ref, o_ref):
    del meta_ref
    o_ref[...] = jnp.zeros(o_ref.shape, U32)


def _dispatch(dest, u2p, meta, n_rows):
    n_tiles, per_tile = dest.shape
    td = per_tile // TOP_K
    blk_rows = EXPERT_ROWS * ROW_CHUNKS
    xs0 = pl.pallas_call(
        _zfill_kernel,
        grid_spec=pltpu.PrefetchScalarGridSpec(
            num_scalar_prefetch=1, grid=(N_EXPERTS,), in_specs=[],
            out_specs=pl.BlockSpec((blk_rows, 128), lambda e, mt: (mt[e], 0))),
        out_shape=jax.ShapeDtypeStruct((n_rows * ROW_CHUNKS, 128), U32),
        compiler_params=_cparams(("arbitrary",)),
        name="zfill",
    )(meta)
    return pl.pallas_call(
        _dispatch_kernel,
        grid=(n_tiles,),
        in_specs=[pl.BlockSpec(memory_space=pl.ANY), pl.BlockSpec((td * ROW_CHUNKS, 128), lambda i: (i, 0)),
                  pl.BlockSpec(memory_space=pl.ANY)],
        out_specs=pl.BlockSpec(memory_space=pl.ANY),
        out_shape=jax.ShapeDtypeStruct(xs0.shape, U32),
        scratch_shapes=[pltpu.SMEM((per_tile,), I32), pltpu.SMEM((per_tile,), I32),
                        pltpu.SemaphoreType.DMA((2,)), pltpu.SemaphoreType.DMA],
        input_output_aliases={2: 0},
        compiler_params=_cparams(("arbitrary",)),
        name="dispatch",
    )(dest, u2p, xs0)


def _used_block(i, meta_ref):
    return jnp.minimum(i, meta_ref[N_EXPERTS] - 1)


def _expert_kernel(blke_ref, meta_ref, xs_ref, wug_ref, bug_ref, wd_ref, bd_ref, ys_ref, wug_bf, wd_bf):
    i = pl.program_id(0)
    f = wd_ref.shape[1]
    cur = blke_ref[_used_block(i, meta_ref)]
    prev = blke_ref[_used_block(jnp.maximum(i - 1, 0), meta_ref)]

    @pl.when((i == 0) | (cur != prev))
    def _():
        wug_bf[...] = wug_ref[0].astype(BF16)
        wd_bf[...] = wd_ref[0].astype(BF16)

    @pl.when(i < meta_ref[N_EXPERTS])
    def _():
        x = _load_rows(xs_ref, EXPERT_ROWS).astype(BF16)
        h = _dot(x, wug_bf[...]) + bug_ref[0]
        a = jnp.minimum(h[:, :f], SWIGLU_LIMIT)
        lin = jnp.clip(h[:, f:], -SWIGLU_LIMIT, SWIGLU_LIMIT)
        act = (a * _sigmoid(SWIGLU_ALPHA * a) * (lin + 1.0)).astype(BF16)
        y = _dot(act, wd_bf[...]) + bd_ref[0]
        _store_rows(ys_ref, y)


def _experts(blke, meta, xs, w_up_gate, b_up_gate, w_down, b_down):
    e, d, f2 = w_up_gate.shape
    blk_rows = EXPERT_ROWS * ROW_CHUNKS
    n_blocks = xs.shape[0] // blk_rows
    row_blk = lambda i, be, mt: (_used_block(i, mt), 0)
    expert = lambda i, be, mt: (be[_used_block(i, mt)], 0, 0)
    return pl.pallas_call(
        _expert_kernel,
        grid_spec=pltpu.PrefetchScalarGridSpec(
            num_scalar_prefetch=2,
            grid=(n_blocks,),
            in_specs=[
                pl.BlockSpec((blk_rows, 128), row_blk),
                pl.BlockSpec((1, d, f2), expert),
                pl.BlockSpec((1, 1, f2), expert),
                pl.BlockSpec((1, f2 // 2, d), expert),
                pl.BlockSpec((1, 1, d), expert),
            ],
            out_specs=pl.BlockSpec((blk_rows, 128), row_blk),
            scratch_shapes=[pltpu.VMEM((d, f2), BF16), pltpu.VMEM((f2 // 2, d), BF16)],
        ),
        out_shape=jax.ShapeDtypeStruct(xs.shape, U32),
        compiler_params=_cparams(("arbitrary",)),
        name="experts",
    )(blke, meta, xs, w_up_gate, b_up_gate.reshape(e, 1, f2), w_down, b_down.reshape(e, 1, d))


def _combine_kernel(dest_hbm, ys_hbm, h1_ref, topg_ref, o_ref, *scratch):
    i = pl.program_id(0)
    n_tiles = pl.num_programs(0) - 1
    tc = h1_ref.shape[0]
    dest_smem = scratch[0:2]
    bufs = (scratch[2:2 + TOP_K], scratch[2 + TOP_K:2 + 2 * TOP_K])
    sem_idx, sems = scratch[2 + 2 * TOP_K], scratch[3 + 2 * TOP_K]

    def row_copy(slot, indexed):
        def make(t, k):
            src = pl.multiple_of(dest_smem[slot][k * tc + t], ROW_CHUNKS) if indexed else 0
            dst = pl.multiple_of(t * ROW_CHUNKS, ROW_CHUNKS)
            return pltpu.make_async_copy(ys_hbm.at[pl.ds(src, ROW_CHUNKS)],
                                         bufs[slot][k].at[pl.ds(dst, ROW_CHUNKS)], sems.at[slot])
        return make

    def start(slot):
        _row_dma_start(tc, row_copy(slot, True))

    def finish(slot):
        _row_dma_wait(tc, row_copy(slot, False))
        g = jnp.concatenate([topg_ref[...], jnp.zeros((8 - TOP_K, tc), F32)], axis=0).T
        out = h1_ref[...]
        for k in range(TOP_K):
            out = out + g[:, k:k + 1] * _load_rows(bufs[slot][k], tc)
        o_ref[...] = out

    _dest_tables(dest_hbm, dest_smem, sem_idx, n_tiles, start)

    for slot in range(2):
        @pl.when((i % 2 != slot) & (i > 0))
        def _():
            finish(slot)


def _combine(dest, ys, h1, topg):
    n_tiles, per_tile = dest.shape
    tc = per_tile // TOP_K
    t, d = h1.shape
    prev = lambda i: jnp.maximum(i - 1, 0)
    return pl.pallas_call(
        _combine_kernel,
        grid=(n_tiles + 1,),
        in_specs=[pl.BlockSpec(memory_space=pl.ANY), pl.BlockSpec(memory_space=pl.ANY),
                  pl.BlockSpec((tc, d), lambda i: (prev(i), 0)), pl.BlockSpec((TOP_K, tc), lambda i: (0, prev(i)))],
        out_specs=pl.BlockSpec((tc, d), lambda i: (prev(i), 0)),
        out_shape=jax.ShapeDtypeStruct((t, d), F32),
        scratch_shapes=[pltpu.SMEM((per_tile,), I32)] * 2 + [pltpu.VMEM((tc * ROW_CHUNKS, 128), U32)] * (2 * TOP_K)
                       + [pltpu.SemaphoreType.DMA((2,)), pltpu.SemaphoreType.DMA((2,))],
        compiler_params=_cparams(("arbitrary",)),
        name="combine",
    )(dest, ys, h1, topg)


def _tile(n, pref):
    return pref if n % pref == 0 else n


def _layer(h, norm1_g, w_in, b_gates, conv_dw, conv_dw_b, conv_ln_g, conv_ln_b, w_conv_out,
           q_norm_g, k_norm_g, w_attn_out, w_out, norm2_g, w_router, b_router,
           w_up_gate, b_up_gate, w_down, b_down):
    b, s, d = h.shape
    t = b * s
    assert s % MOBA_BLOCK == 0 and s // MOBA_BLOCK <= GATE_SLOTS
    tm = _tile(s, 512)
    x2 = h.reshape(t, d)
    hglu, q, k, v, gates, gt = _inproj(x2, norm1_g, w_in, b_gates, q_norm_g, k_norm_g, s, tm)
    c = hglu.shape[1]
    aw = q.shape[1]
    conv = _conv(hglu.reshape(b, s, c), conv_dw, conv_dw_b, conv_ln_g, conv_ln_b, tm)
    attn = _attention(q.reshape(b, s, aw), k.reshape(b, s, N_HEADS * K_AUG), v.reshape(b, s, aw), gt, tm)
    h1, u2p, topi, topg = _mixout(x2, conv.reshape(t, c), attn.reshape(t, aw), gates, w_conv_out, w_attn_out,
                                  w_out, norm2_g, w_router, b_router, tm)
    n_blocks = -(-(t * TOP_K) // EXPERT_ROWS) + N_EXPERTS
    rank, cnt = _rank(topi, _tile(t, 1024))
    dest, blke, meta = _dest(topi, rank, cnt, tm, n_blocks)
    dest = dest.reshape(t // tm, TOP_K * tm)
    meta = meta.reshape(128)
    xs = _dispatch(dest, u2p, meta, n_blocks * EXPERT_ROWS)
    ys = _experts(blke.reshape(n_blocks), meta, xs, w_up_gate, b_up_gate, w_down, b_down)
    out = _combine(dest, ys, h1, topg)
    return out.reshape(b, s, d)


def kernel(x, norm1_g, w_in, b_gates, conv_dw, conv_dw_b, conv_ln_g, conv_ln_b, w_conv_out, q_norm_g, k_norm_g,
           w_attn_out, w_out, norm2_g, w_router, b_router, w_up_gate, b_up_gate, w_down, b_down):
    h = x
    for l in range(norm1_g.shape[0]):
        h = _layer(h, norm1_g[l], w_in[l], b_gates[l], conv_dw[l], conv_dw_b[l], conv_ln_g[l], conv_ln_b[l],
                   w_conv_out[l], q_norm_g[l], k_norm_g[l], w_attn_out[l], w_out[l], norm2_g[l], w_router[l],
                   b_router[l], w_up_gate[l], b_up_gate[l], w_down[l], b_down[l])
    return h
```

```python
import functools

import jax
import jax.numpy as jnp
from jax import lax
from jax.experimental import pallas as pl
from jax.experimental.pallas import tpu as pltpu

F32 = jnp.float32
BF16 = jnp.bfloat16
I32 = jnp.int32
U32 = jnp.uint32

EPS = 1e-6
CONV_WIDTH = 31
HALO = 32
N_HEADS = 8
HEAD_DIM = 64
MOBA_BLOCK = 256
MOBA_TOPK = 3
GATE_SLOTS = 16
HEADS_PER_GROUP = 4
K_AUG = 128
AUG_POS = 16
V_ROWS = 80
MASKED = -1e30
LOG2E = 1.4426950408889634
M_INIT = -3e38
ATTN_UNROLL = 8
ATTN_LOOKAHEAD = 6
ATTN_SLOTS = 8
N_EXPERTS = 32
TOP_K = 4
SWIGLU_LIMIT = 7.0
SWIGLU_ALPHA = 1.702
EXPERT_ROWS = 512
ROW_CHUNKS = 4
DMA_UNROLL = 8
NEG_INF = float("-inf")
VMEM_LIMIT = 56 * 1024 * 1024


def _cparams(sem):
    return pltpu.CompilerParams(dimension_semantics=sem, vmem_limit_bytes=VMEM_LIMIT)


def _dot(a, b):
    return jnp.dot(a, b, preferred_element_type=F32)


def _dot_nt(a, b):
    return lax.dot_general(a, b, (((1,), (1,)), ((), ())), preferred_element_type=F32)


def _split_bf16(x):
    hi = x.astype(BF16)
    lo = (x - hi.astype(F32)).astype(BF16)
    return hi, lo


def _sigmoid(x):
    return 1.0 / (1.0 + jnp.exp(-x))


def _inproj_kernel(tiles_per_seq, x_ref, g1_ref, wglu_ref, wq_ref, wk_ref, wv_ref, wg_ref, bg_ref,
                   qg_ref, kg_ref, bd_ref, place_ref,
                   h_ref, q_ref, k_ref, v_ref, gate_ref, gt_ref, km_ref):
    i = pl.program_id(0)
    tm = x_ref.shape[0]
    c = h_ref.shape[1]
    aw = q_ref.shape[1]

    @pl.when(i == 0)
    def _():
        km_ref[...] = jnp.zeros_like(km_ref)

    x = x_ref[...]
    ms = jnp.mean(x * x, axis=-1, keepdims=True)
    u = (x * lax.rsqrt(ms + EPS) * g1_ref[...]).astype(BF16)

    glu = _dot(u, wglu_ref[...])
    h_ref[...] = (glu[:, :c] * _sigmoid(glu[:, c:])).astype(BF16)

    bd = bd_ref[...]

    def head_norm(t, g):
        ss = _dot((t * t).astype(BF16), bd)
        return t * lax.rsqrt(ss * (1.0 / HEAD_DIM) + EPS) * g

    qn = head_norm(_dot(u, wq_ref[...]), qg_ref[...]) * (HEAD_DIM ** -0.5 * LOG2E)
    kn = head_norm(_dot(u, wk_ref[...]), kg_ref[...])
    q_hi, q_lo = _split_bf16(qn)
    q_ref[...] = q_hi
    pos = lax.broadcasted_iota(I32, (tm, K_AUG), 0) + (i % tiles_per_seq) * tm
    col = lax.broadcasted_iota(I32, (tm, K_AUG), 1)
    kpos = pos & (MOBA_BLOCK - 1)
    bstart = pos - kpos
    aug = jnp.where((col >= HEAD_DIM) & (col < HEAD_DIM + 3), kpos,
                    jnp.where((col >= HEAD_DIM + 3) & (col < HEAD_DIM + 6), bstart,
                              jnp.where((col - (HEAD_DIM + AUG_POS)) * MOBA_BLOCK == bstart, 1, 0))).astype(F32)
    kp = _dot(kn.astype(BF16), place_ref[...])
    for h in range(N_HEADS):
        k_ref[:, h * K_AUG:(h + 1) * K_AUG] = (kp[:, h * K_AUG:(h + 1) * K_AUG] + aug).astype(BF16)
    v_ref[...] = _dot(u, wv_ref[...]).astype(BF16)
    gate_ref[...] = _sigmoid(_dot(u, wg_ref[...]) + bg_ref[...]).astype(BF16)

    lane_head = lax.broadcasted_iota(I32, (1, aw), 1) // HEAD_DIM
    blocks_per_tile = tm // MOBA_BLOCK
    n0 = (i % tiles_per_seq) * blocks_per_tile
    for j in range(blocks_per_tile):
        kmj = jnp.mean(kn[j * MOBA_BLOCK:(j + 1) * MOBA_BLOCK], axis=0, keepdims=True)
        for h in range(N_HEADS):
            km_ref[pl.ds(h * GATE_SLOTS + n0 + j, 1), :] = jnp.where(lane_head == h, kmj, 0.0)

    km_hi, km_lo = _split_bf16(km_ref[...])
    gt_ref[0] = _dot_nt(km_hi, q_hi) + _dot_nt(km_lo, q_hi) + _dot_nt(km_hi, q_lo)


def _inproj(x2, norm1_g, w_in, b_gates, q_norm_g, k_norm_g, seq, tm):
    t, d = x2.shape
    aw = N_HEADS * HEAD_DIM
    c = (w_in.shape[1] - 3 * aw - 2 * d) // 2
    c1, c2, c3, c4 = 2 * c, 2 * c + aw, 2 * c + 2 * aw, 2 * c + 3 * aw
    wb = w_in.astype(BF16)
    hd = jnp.arange(aw, dtype=I32) // HEAD_DIM
    bd = (hd[:, None] == hd[None, :]).astype(BF16)
    lane = jnp.arange(aw, dtype=I32)
    place = ((lane // HEAD_DIM * K_AUG + lane % HEAD_DIM)[:, None]
             == jnp.arange(N_HEADS * K_AUG, dtype=I32)[None, :]).astype(BF16)
    n_tiles = t // tm
    full = lambda a: pl.BlockSpec(a.shape, lambda i: (0,) * a.ndim)
    row = lambda n: pl.BlockSpec((tm, n), lambda i: (i, 0))
    args = (x2, norm1_g.reshape(1, d), wb[:, :c1], wb[:, c1:c2], wb[:, c2:c3], wb[:, c3:c4], wb[:, c4:],
            b_gates.reshape(1, 2 * d), jnp.tile(q_norm_g, N_HEADS).reshape(1, aw),
            jnp.tile(k_norm_g, N_HEADS).reshape(1, aw), bd, place)
    return pl.pallas_call(
        functools.partial(_inproj_kernel, seq // tm),
        grid=(n_tiles,),
        in_specs=[row(d)] + [full(a) for a in args[1:]],
        out_specs=[row(c), row(aw), row(N_HEADS * K_AUG), row(aw), row(2 * d),
                   pl.BlockSpec((1, N_HEADS * GATE_SLOTS, tm), lambda i: (i, 0, 0))],
        out_shape=[jax.ShapeDtypeStruct((t, c), BF16), jax.ShapeDtypeStruct((t, aw), BF16),
                   jax.ShapeDtypeStruct((t, N_HEADS * K_AUG), BF16), jax.ShapeDtypeStruct((t, aw), BF16),
                   jax.ShapeDtypeStruct((t, 2 * d), BF16),
                   jax.ShapeDtypeStruct((n_tiles, N_HEADS * GATE_SLOTS, tm), F32)],
        scratch_shapes=[pltpu.VMEM((N_HEADS * GATE_SLOTS, aw), F32)],
        compiler_params=_cparams(("arbitrary",)),
        name="inproj",
    )(*args)


def _conv_kernel(h_ref, dw_ref, dwb_ref, lng_ref, lnb_ref, o_ref, win_ref, sh_ref):
    s = pl.program_id(1)
    ts = h_ref.shape[1]

    @pl.when(s == 0)
    def _():
        win_ref[0:HALO, :] = jnp.zeros((HALO, win_ref.shape[1]), F32)

    @pl.when(s != 0)
    def _():
        win_ref[0:HALO, :] = win_ref[ts:ts + HALO, :]

    win_ref[HALO:HALO + ts, :] = h_ref[0].astype(F32)
    span = ts + HALO - 8
    for r in range(1, 8):
        sh_ref[r - 1] = win_ref[r:r + span, :]
    acc = jnp.zeros((ts, win_ref.shape[1]), F32) + dwb_ref[...]
    for w in range(CONV_WIDTH):
        off = HALO - (CONV_WIDTH - 1) + w
        r = off % 8
        tap = win_ref[off:off + ts, :] if r == 0 else sh_ref[r - 1, off - r:off - r + ts, :]
        acc = acc + dw_ref[w:w + 1, :] * tap
    mu = jnp.mean(acc, axis=-1, keepdims=True)
    cen = acc - mu
    var = jnp.mean(cen * cen, axis=-1, keepdims=True)
    y = cen * lax.rsqrt(var + EPS) * lng_ref[...] + lnb_ref[...]
    o_ref[0] = (y * _sigmoid(y)).astype(BF16)


def _conv(h3, dw, dwb, lng, lnb, ts):
    b, s, c = h3.shape
    vec = lambda a: pl.BlockSpec(a.shape, lambda i, j: (0, 0))
    args = (h3, dw, dwb.reshape(1, c), lng.reshape(1, c), lnb.reshape(1, c))
    return pl.pallas_call(
        _conv_kernel,
        grid=(b, s // ts),
        in_specs=[pl.BlockSpec((1, ts, c), lambda i, j: (i, j, 0))] + [vec(a) for a in args[1:]],
        out_specs=pl.BlockSpec((1, ts, c), lambda i, j: (i, j, 0)),
        out_shape=jax.ShapeDtypeStruct((b, s, c), BF16),
        scratch_shapes=[pltpu.VMEM((ts + HALO, c), F32), pltpu.VMEM((7, ts + HALO - 8, c), F32)],
        compiler_params=_cparams(("arbitrary", "arbitrary")),
        name="conv",
    )(*args)


def _attn_kernel(q_ref, k_ref, v_ref, gt_ref, srow_ref, o_ref,
                 vt_ref, caus_ref, w_ref, s_ref, m_ref, acc_ref):
    qb = pl.program_id(2)
    blk = MOBA_BLOCK
    hp = HEADS_PER_GROUP
    n_blocks = v_ref.shape[1] // blk

    @pl.when(qb == 0)
    def _():
        ones_rows = (lax.broadcasted_iota(I32, (V_ROWS - HEAD_DIM, blk), 0) == 0).astype(BF16)
        for n in range(n_blocks):
            vt = v_ref[0, n * blk:(n + 1) * blk, :].astype(F32).T
            for h in range(hp):
                vt_ref[n, h, 0:HEAD_DIM, :] = vt[h * HEAD_DIM:(h + 1) * HEAD_DIM].astype(BF16)
                vt_ref[n, h, HEAD_DIM:V_ROWS, :] = ones_rows
        kpos = lax.broadcasted_iota(I32, (blk, blk), 0)
        qpos = lax.broadcasted_iota(I32, (blk, blk), 1)
        caus_ref[0] = jnp.zeros((blk, blk), F32)
        caus_ref[1] = jnp.where(kpos <= qpos, 0.0, NEG_INF)
        for h in range(hp):
            w_ref[h, HEAD_DIM:HEAD_DIM + AUG_POS, :] = srow_ref[h]
            w_ref[h, HEAD_DIM + AUG_POS + GATE_SLOTS:K_AUG, :] = jnp.zeros(
                (K_AUG - HEAD_DIM - AUG_POS - GATE_SLOTS, blk), BF16)

    qt = q_ref[0].astype(F32).T
    slot = lax.broadcasted_iota(I32, (GATE_SLOTS, blk), 0)
    for h in range(hp):
        g = gt_ref[0, h * GATE_SLOTS:(h + 1) * GATE_SLOTS, :]
        g = jnp.where(slot < qb, g, NEG_INF)
        sel = slot == qb
        for _ in range(MOBA_TOPK):
            mx = jnp.max(g, axis=0, keepdims=True)
            first = jnp.min(jnp.where(g == mx, slot, GATE_SLOTS), axis=0, keepdims=True)
            pick = (slot == first) & (mx > NEG_INF)
            sel = sel | pick
            g = jnp.where(pick, NEG_INF, g)
        w_ref[h, 0:HEAD_DIM, :] = qt[h * HEAD_DIM:(h + 1) * HEAD_DIM].astype(BF16)
        w_ref[h, HEAD_DIM + AUG_POS:HEAD_DIM + AUG_POS + GATE_SLOTS, :] = jnp.where(sel, 0.0, MASKED).astype(BF16)

    m_ref[...] = jnp.full(m_ref.shape, M_INIT, F32)
    acc_ref[...] = jnp.zeros(acc_ref.shape, F32)

    def scores(n, h, slot):
        start = pl.multiple_of(n * blk, blk)
        t = _dot(k_ref[0, pl.ds(start, blk), h * K_AUG:(h + 1) * K_AUG], w_ref[h])
        s_ref[slot] = t + caus_ref[jnp.where(n == qb, 1, 0)]

    def accumulate(n, h, slot):
        t = s_ref[slot]
        m_old = m_ref[h]
        m_new = jnp.maximum(m_old, jnp.max(t, axis=0, keepdims=True))
        p = jnp.exp2(t - m_new).astype(BF16)
        pv = _dot(vt_ref[n, h], p)
        m_ref[h] = m_new
        acc_ref[h] = jnp.exp2(m_old - m_new) * acc_ref[h] + pv

    def run_blocks(first, count):
        units = [(first + u, h) for u in range(count) for h in range(hp)]
        ahead = min(ATTN_LOOKAHEAD, len(units))
        for j in range(ahead):
            scores(*units[j], j % ATTN_SLOTS)
        for j, (n, h) in enumerate(units):
            if j + ahead < len(units):
                scores(*units[j + ahead], (j + ahead) % ATTN_SLOTS)
            accumulate(n, h, j % ATTN_SLOTS)

    def chunk(c, carry):
        run_blocks(c * ATTN_UNROLL, ATTN_UNROLL)
        return carry

    full = (qb + 1) // ATTN_UNROLL
    lax.fori_loop(0, full, chunk, 0)
    for left in range(1, ATTN_UNROLL):
        @pl.when(qb + 1 - full * ATTN_UNROLL == left)
        def _():
            run_blocks(full * ATTN_UNROLL, left)
    outs = [acc_ref[h, 0:HEAD_DIM, :] / acc_ref[h, HEAD_DIM:HEAD_DIM + 1, :] for h in range(hp)]
    o_ref[0] = jnp.concatenate(outs, axis=0).T.astype(BF16)


def _slope_rows():
    slopes = 2.0 ** (-8.0 * jnp.arange(1, N_HEADS + 1, dtype=F32) / N_HEADS) * LOG2E
    s1 = slopes.astype(BF16)
    r1 = slopes - s1.astype(F32)
    s2 = r1.astype(BF16)
    s3 = (r1 - s2.astype(F32)).astype(BF16)
    parts = jnp.stack([s1, s2, s3, s1, s2, s3] + [jnp.zeros_like(s1)] * (AUG_POS - 6), axis=1)
    return jnp.broadcast_to(parts[:, :, None], (N_HEADS, AUG_POS, MOBA_BLOCK))


def _attention(q3, kaug3, v3, gt, tm):
    b, s, aw = q3.shape
    blk = MOBA_BLOCK
    hp = HEADS_PER_GROUP
    gw = hp * HEAD_DIM
    n_groups = aw // gw
    nb = s // blk
    assert nb <= GATE_SLOTS
    per_tile = tm // blk
    tiles_per_seq = s // tm
    return pl.pallas_call(
        _attn_kernel,
        grid=(b, n_groups, nb),
        in_specs=[
            pl.BlockSpec((1, blk, gw), lambda i, g, j: (i, j, g)),
            pl.BlockSpec((1, s, hp * K_AUG), lambda i, g, j: (i, 0, g)),
            pl.BlockSpec((1, s, gw), lambda i, g, j: (i, 0, g)),
            pl.BlockSpec((1, hp * GATE_SLOTS, blk),
                         lambda i, g, j: (i * tiles_per_seq + j // per_tile, g, j % per_tile)),
            pl.BlockSpec((hp, AUG_POS, blk), lambda i, g, j: (g, 0, 0)),
        ],
        out_specs=pl.BlockSpec((1, blk, gw), lambda i, g, j: (i, j, g)),
        scratch_shapes=[
            pltpu.VMEM((nb, hp, V_ROWS, blk), BF16),
            pltpu.VMEM((2, blk, blk), F32),
            pltpu.VMEM((hp, K_AUG, blk), BF16),
            pltpu.VMEM((ATTN_SLOTS, blk, blk), F32),
            pltpu.VMEM((hp, 1, blk), F32),
            pltpu.VMEM((hp, V_ROWS, blk), F32),
        ],
        out_shape=jax.ShapeDtypeStruct((b, s, aw), BF16),
        compiler_params=_cparams(("arbitrary", "arbitrary", "arbitrary")),
        name="attn",
    )(q3, kaug3, v3, gt, _slope_rows())


def _pack_bf16_pairs(x):
    n = x.shape[1] // 2
    r = x.astype(BF16).astype(F32)
    lo = lax.shift_right_logical(pltpu.bitcast(r[:, :n], U32), jnp.uint32(16))
    hi = pltpu.bitcast(r[:, n:], U32) & jnp.uint32(0xFFFF0000)
    return hi | lo


def _store_rows(ref, x):
    pk = _pack_bf16_pairs(x)
    m, n = pk.shape
    c = n // 128
    for j in range(c):
        ref[pl.ds(j, m, stride=c), :] = pk[:, j * 128:(j + 1) * 128]


def _load_rows(ref, m):
    c = ref.shape[0] // m
    ws = [ref[pl.ds(j, m, stride=c), :] for j in range(c)]
    lo = [pltpu.bitcast(lax.shift_left(w, jnp.uint32(16)), F32) for w in ws]
    hi = [pltpu.bitcast(w & jnp.uint32(0xFFFF0000), F32) for w in ws]
    return jnp.concatenate(lo + hi, axis=1)


def _mixout_kernel(x_ref, c_ref, a_ref, gate_ref, wco_ref, wao_ref, wout_ref, g2_ref,
                   wr_ref, br_ref,
                   h1_ref, u2_ref, topi_ref, topg_ref):
    d = x_ref.shape[1]
    y_c = _dot(c_ref[...], wco_ref[...])
    y_a = _dot(a_ref[...], wao_ref[...])
    gate = gate_ref[...].astype(F32)
    mixed = (gate[:, :d] * y_c + gate[:, d:] * y_a).astype(BF16)
    h1 = x_ref[...] + _dot(mixed, wout_ref[...])
    h1_ref[...] = h1
    ms = jnp.mean(h1 * h1, axis=-1, keepdims=True)
    u2 = h1 * lax.rsqrt(ms + EPS) * g2_ref[...]
    _store_rows(u2_ref, u2)
    u_hi, u_lo = _split_bf16(u2)
    wr = wr_ref[...]
    both = _dot_nt(wr, u_hi)
    logits = both[:N_EXPERTS] + both[N_EXPERTS:] + _dot_nt(wr[:N_EXPERTS], u_lo) + br_ref[...]
    eid = lax.broadcasted_iota(I32, logits.shape, 0)
    vals = []
    for k in range(TOP_K):
        mx = jnp.max(logits, axis=0, keepdims=True)
        first = jnp.min(jnp.where(logits == mx, eid, N_EXPERTS), axis=0, keepdims=True)
        topi_ref[k:k + 1, :] = first
        vals.append(mx)
        logits = jnp.where(eid == first, NEG_INF, logits)
    es = [jnp.exp(v - vals[0]) for v in vals]
    den = es[0] + es[1] + es[2] + es[3]
    for k in range(TOP_K):
        topg_ref[k:k + 1, :] = es[k] / den


def _mixout(x2, c2, a2, gates, w_conv_out, w_attn_out, w_out, norm2_g, w_router, b_router, tm):
    t, d = x2.shape
    wr_t = w_router.T
    wrh = wr_t.astype(BF16)
    wrl = (wr_t - wrh.astype(F32)).astype(BF16)
    args = (x2, c2, a2, gates, w_conv_out.astype(BF16), w_attn_out.astype(BF16), w_out.astype(BF16),
            norm2_g.reshape(1, d), jnp.concatenate([wrh, wrl], axis=0), b_router.reshape(N_EXPERTS, 1))
    full = lambda a: pl.BlockSpec(a.shape, lambda i: (0,) * a.ndim)
    row = lambda a: pl.BlockSpec((tm, a.shape[1]), lambda i: (i, 0))
    return pl.pallas_call(
        _mixout_kernel,
        grid=(t // tm,),
        in_specs=[row(a) for a in args[:4]] + [full(a) for a in args[4:]],
        out_specs=[pl.BlockSpec((tm, d), lambda i: (i, 0)), pl.BlockSpec((tm * ROW_CHUNKS, 128), lambda i: (i, 0)),
                   pl.BlockSpec((TOP_K, tm), lambda i: (0, i)), pl.BlockSpec((TOP_K, tm), lambda i: (0, i))],
        out_shape=[jax.ShapeDtypeStruct((t, d), F32), jax.ShapeDtypeStruct((t * ROW_CHUNKS, 128), U32),
                   jax.ShapeDtypeStruct((TOP_K, t), I32), jax.ShapeDtypeStruct((TOP_K, t), F32)],
        compiler_params=_cparams(("arbitrary",)),
        name="mixout",
    )(*args)


def _rank_kernel(topi_ref, tri_ref, rank_ref, cnt_ref, run_ref):
    i = pl.program_id(0)
    tr = topi_ref.shape[1]

    @pl.when(i == 0)
    def _():
        run_ref[...] = jnp.zeros_like(run_ref)

    eid = lax.broadcasted_iota(I32, (N_EXPERTS, tr), 0)
    onehots = [(eid == topi_ref[k:k + 1, :]).astype(F32) for k in range(TOP_K)]
    tot = onehots[0] + onehots[1] + onehots[2] + onehots[3]
    before = _dot(tot.astype(BF16), tri_ref[...])
    base = run_ref[...] + before
    for k in range(TOP_K):
        rank_ref[k:k + 1, :] = jnp.sum(onehots[k] * base, axis=0, keepdims=True).astype(I32)
    run_ref[...] = run_ref[...] + jnp.sum(tot, axis=1, keepdims=True)
    cnt_ref[...] = run_ref[...]


def _rank(topi, tr):
    t = topi.shape[1]
    tri = (jnp.arange(tr)[:, None] < jnp.arange(tr)[None, :]).astype(BF16)
    return pl.pallas_call(
        _rank_kernel,
        grid=(t // tr,),
        in_specs=[pl.BlockSpec((TOP_K, tr), lambda i: (0, i)), pl.BlockSpec((tr, tr), lambda i: (0, 0))],
        out_specs=[pl.BlockSpec((TOP_K, tr), lambda i: (0, i)), pl.BlockSpec((N_EXPERTS, 1), lambda i: (0, 0))],
        out_shape=[jax.ShapeDtypeStruct((TOP_K, t), I32), jax.ShapeDtypeStruct((N_EXPERTS, 1), F32)],
        scratch_shapes=[pltpu.VMEM((N_EXPERTS, 1), F32)],
        compiler_params=_cparams(("arbitrary",)),
        name="rank",
    )(topi, tri)


def _dest_kernel(n_blocks, topi_ref, rank_ref, cnt_ref, dest_ref, blke_ref, meta_ref):
    n_sub, _, td = dest_ref.shape
    cnt = cnt_ref[...]
    padded = jnp.floor((cnt + (EXPERT_ROWS - 1)) * (1.0 / EXPERT_ROWS)) * EXPERT_ROWS
    er = lax.broadcasted_iota(I32, (N_EXPERTS, N_EXPERTS), 0)
    ec = lax.broadcasted_iota(I32, (N_EXPERTS, N_EXPERTS), 1)
    pad_row = jnp.sum(jnp.where(er == ec, padded, 0.0), axis=0, keepdims=True)
    start = jnp.sum(jnp.where(ec < er, pad_row, 0.0), axis=1, keepdims=True)
    end = start + padded
    eid = lax.broadcasted_iota(I32, (N_EXPERTS, td), 0)
    for j in range(n_sub):
        for k in range(TOP_K):
            oh = eid == topi_ref[k:k + 1, j * td:(j + 1) * td]
            s_k = jnp.sum(jnp.where(oh, start, 0.0), axis=0, keepdims=True)
            dest_ref[j, k:k + 1, :] = (s_k.astype(I32) + rank_ref[k:k + 1, j * td:(j + 1) * td]) * ROW_CHUNKS
    bstart = (lax.broadcasted_iota(I32, (N_EXPERTS, n_blocks), 1) * EXPERT_ROWS).astype(F32)
    be = jnp.sum((end <= bstart).astype(I32), axis=0, keepdims=True)
    blke_ref[...] = jnp.minimum(be, N_EXPERTS - 1)
    er2 = lax.broadcasted_iota(I32, (N_EXPERTS, 128), 0)
    lane = lax.broadcasted_iota(I32, (N_EXPERTS, 128), 1)
    last = jnp.maximum(end * (1.0 / EXPERT_ROWS) - 1.0, 0.0)
    used = jnp.where(er2 == N_EXPERTS - 1, end * (1.0 / EXPERT_ROWS), 0.0)
    meta = jnp.where(lane == er2, last, 0.0) + jnp.where(lane == N_EXPERTS, used, 0.0)
    meta_ref[...] = jnp.sum(meta, axis=0, keepdims=True).astype(I32)


def _dest(topi, rank, cnt, td, n_blocks):
    t = topi.shape[1]
    n_sub = 4 if t % (4 * td) == 0 else 1
    wide = n_sub * td
    return pl.pallas_call(
        functools.partial(_dest_kernel, n_blocks),
        grid=(t // wide,),
        in_specs=[pl.BlockSpec((TOP_K, wide), lambda i: (0, i)), pl.BlockSpec((TOP_K, wide), lambda i: (0, i)),
                  pl.BlockSpec((N_EXPERTS, 1), lambda i: (0, 0))],
        out_specs=[pl.BlockSpec((n_sub, TOP_K, td), lambda i: (i, 0, 0)),
                   pl.BlockSpec((1, n_blocks), lambda i: (0, 0)), pl.BlockSpec((1, 128), lambda i: (0, 0))],
        out_shape=[jax.ShapeDtypeStruct((t // td, TOP_K, td), I32), jax.ShapeDtypeStruct((1, n_blocks), I32),
                   jax.ShapeDtypeStruct((1, 128), I32)],
        compiler_params=_cparams(("arbitrary",)),
        name="dest",
    )(topi, rank, cnt)


def _row_dma_start(n_tok, row_copy):
    def issue(it, carry):
        for u in range(DMA_UNROLL):
            for k in range(TOP_K):
                row_copy(it * DMA_UNROLL + u, k).start(priority=k % 2)
        return carry

    lax.fori_loop(0, n_tok // DMA_UNROLL, issue, 0)


def _row_dma_wait(n_tok, row_copy):
    def drain(it, carry):
        for u in range(DMA_UNROLL):
            for k in range(TOP_K):
                row_copy(it * DMA_UNROLL + u, k).wait()
        return carry

    lax.fori_loop(0, n_tok // DMA_UNROLL, drain, 0)


def _dest_tables(dest_hbm, tables, sem_idx, n_tiles, use):
    i = pl.program_id(0)
    fetch = lambda tile, slot: pltpu.make_async_copy(dest_hbm.at[tile], tables[slot], sem_idx.at[slot])

    @pl.when(i == 0)
    def _():
        fetch(0, 0).start()

    for slot in range(2):
        @pl.when((i % 2 == slot) & (i < n_tiles))
        def _():
            fetch(i, slot).wait()

            @pl.when(i + 1 < n_tiles)
            def _():
                fetch(i + 1, 1 - slot).start()

            use(slot)


def _dispatch_kernel(dest_hbm, u2_ref, xs_in, xs_hbm, dest_a, dest_b, sem_idx, sem):
    del xs_in
    td = u2_ref.shape[0] // ROW_CHUNKS
    tables = (dest_a, dest_b)

    def scatter(slot):
        def row_copy(t, k):
            src = pl.multiple_of(t * ROW_CHUNKS, ROW_CHUNKS)
            dst = pl.multiple_of(tables[slot][k * td + t], ROW_CHUNKS)
            return pltpu.make_async_copy(u2_ref.at[pl.ds(src, ROW_CHUNKS)], xs_hbm.at[pl.ds(dst, ROW_CHUNKS)], sem)

        _row_dma_start(td, row_copy)
        _row_dma_wait(td, row_copy)

    _dest_tables(dest_hbm, tables, sem_idx, pl.num_programs(0), scatter)


def _zfill_kernel(meta_ref, o_ref):
    del meta_ref
    o_ref[...] = jnp.zeros(o_ref.shape, U32)


def _dispatch(dest, u2p, meta, n_rows):
    n_tiles, per_tile = dest.shape
    td = per_tile // TOP_K
    blk_rows = EXPERT_ROWS * ROW_CHUNKS
    xs0 = pl.pallas_call(
        _zfill_kernel,
        grid_spec=pltpu.PrefetchScalarGridSpec(
            num_scalar_prefetch=1, grid=(N_EXPERTS,), in_specs=[],
            out_specs=pl.BlockSpec((blk_rows, 128), lambda e, mt: (mt[e], 0))),
        out_shape=jax.ShapeDtypeStruct((n_rows * ROW_CHUNKS, 128), U32),
        compiler_params=_cparams(("arbitrary",)),
        name="zfill",
    )(meta)
    return pl.pallas_call(
        _dispatch_kernel,
        grid=(n_tiles,),
        in_specs=[pl.BlockSpec(memory_space=pl.ANY), pl.BlockSpec((td * ROW_CHUNKS, 128), lambda i: (i, 0)),
                  pl.BlockSpec(memory_space=pl.ANY)],
        out_specs=pl.BlockSpec(memory_space=pl.ANY),
        out_shape=jax.ShapeDtypeStruct(xs0.shape, U32),
        scratch_shapes=[pltpu.SMEM((per_tile,), I32), pltpu.SMEM((per_tile,), I32),
                        pltpu.SemaphoreType.DMA((2,)), pltpu.SemaphoreType.DMA],
        input_output_aliases={2: 0},
        compiler_params=_cparams(("arbitrary",)),
        name="dispatch",
    )(dest, u2p, xs0)


def _used_block(i, meta_ref):
    return jnp.minimum(i, meta_ref[N_EXPERTS] - 1)


def _expert_kernel(blke_ref, meta_ref, xs_ref, wug_ref, bug_ref, wd_ref, bd_ref, ys_ref, wug_bf, wd_bf):
    i = pl.program_id(0)
    f = wd_ref.shape[1]
    cur = blke_ref[_used_block(i, meta_ref)]
    prev = blke_ref[_used_block(jnp.maximum(i - 1, 0), meta_ref)]

    @pl.when((i == 0) | (cur != prev))
    def _():
        wug_bf[...] = wug_ref[0].astype(BF16)
        wd_bf[...] = wd_ref[0].astype(BF16)

    @pl.when(i < meta_ref[N_EXPERTS])
    def _():
        x = _load_rows(xs_ref, EXPERT_ROWS).astype(BF16)
        h = _dot(x, wug_bf[...]) + bug_ref[0]
        a = jnp.minimum(h[:, :f], SWIGLU_LIMIT)
        lin = jnp.clip(h[:, f:], -SWIGLU_LIMIT, SWIGLU_LIMIT)
        act = (a * _sigmoid(SWIGLU_ALPHA * a) * (lin + 1.0)).astype(BF16)
        y = _dot(act, wd_bf[...]) + bd_ref[0]
        _store_rows(ys_ref, y)


def _experts(blke, meta, xs, w_up_gate, b_up_gate, w_down, b_down):
    e, d, f2 = w_up_gate.shape
    blk_rows = EXPERT_ROWS * ROW_CHUNKS
    n_blocks = xs.shape[0] // blk_rows
    row_blk = lambda i, be, mt: (_used_block(i, mt), 0)
    expert = lambda i, be, mt: (be[_used_block(i, mt)], 0, 0)
    return pl.pallas_call(
        _expert_kernel,
        grid_spec=pltpu.PrefetchScalarGridSpec(
            num_scalar_prefetch=2,
            grid=(n_blocks,),
            in_specs=[
                pl.BlockSpec((blk_rows, 128), row_blk),
                pl.BlockSpec((1, d, f2), expert),
                pl.BlockSpec((1, 1, f2), expert),
                pl.BlockSpec((1, f2 // 2, d), expert),
                pl.BlockSpec((1, 1, d), expert),
            ],
            out_specs=pl.BlockSpec((blk_rows, 128), row_blk),
            scratch_shapes=[pltpu.VMEM((d, f2), BF16), pltpu.VMEM((f2 // 2, d), BF16)],
        ),
        out_shape=jax.ShapeDtypeStruct(xs.shape, U32),
        compiler_params=_cparams(("arbitrary",)),
        name="experts",
    )(blke, meta, xs, w_up_gate, b_up_gate.reshape(e, 1, f2), w_down, b_down.reshape(e, 1, d))


def _combine_kernel(dest_hbm, ys_hbm, h1_ref, topg_ref, o_ref, *scratch):
    i = pl.program_id(0)
    n_tiles = pl.num_programs(0) - 1
    tc = h1_ref.shape[0]
    dest_smem = scratch[0:2]
    bufs = (scratch[2:2 + TOP_K], scratch[2 + TOP_K:2 + 2 * TOP_K])
    sem_idx, sems = scratch[2 + 2 * TOP_K], scratch[3 + 2 * TOP_K]

    def row_copy(slot, indexed):
        def make(t, k):
            src = pl.multiple_of(dest_smem[slot][k * tc + t], ROW_CHUNKS) if indexed else 0
            dst = pl.multiple_of(t * ROW_CHUNKS, ROW_CHUNKS)
            return pltpu.make_async_copy(ys_hbm.at[pl.ds(src, ROW_CHUNKS)],
                                         bufs[slot][k].at[pl.ds(dst, ROW_CHUNKS)], sems.at[slot])
        return make

    def start(slot):
        _row_dma_start(tc, row_copy(slot, True))

    def finish(slot):
        _row_dma_wait(tc, row_copy(slot, False))
        g = jnp.concatenate([topg_ref[...], jnp.zeros((8 - TOP_K, tc), F32)], axis=0).T
        out = h1_ref[...]
        for k in range(TOP_K):
            out = out + g[:, k:k + 1] * _load_rows(bufs[slot][k], tc)
        o_ref[...] = out

    _dest_tables(dest_hbm, dest_smem, sem_idx, n_tiles, start)

    for slot in range(2):
        @pl.when((i % 2 != slot) & (i > 0))
        def _():
            finish(slot)


def _combine(dest, ys, h1, topg):
    n_tiles, per_tile = dest.shape
    tc = per_tile // TOP_K
    t, d = h1.shape
    prev = lambda i: jnp.maximum(i - 1, 0)
    return pl.pallas_call(
        _combine_kernel,
        grid=(n_tiles + 1,),
        in_specs=[pl.BlockSpec(memory_space=pl.ANY), pl.BlockSpec(memory_space=pl.ANY),
                  pl.BlockSpec((tc, d), lambda i: (prev(i), 0)), pl.BlockSpec((TOP_K, tc), lambda i: (0, prev(i)))],
        out_specs=pl.BlockSpec((tc, d), lambda i: (prev(i), 0)),
        out_shape=jax.ShapeDtypeStruct((t, d), F32),
        scratch_shapes=[pltpu.SMEM((per_tile,), I32)] * 2 + [pltpu.VMEM((tc * ROW_CHUNKS, 128), U32)] * (2 * TOP_K)
                       + [pltpu.SemaphoreType.DMA((2,)), pltpu.SemaphoreType.DMA((2,))],
        compiler_params=_cparams(("arbitrary",)),
        name="combine",
    )(dest, ys, h1, topg)


def _tile(n, pref):
    return pref if n % pref == 0 else n


def _layer(h, norm1_g, w_in, b_gates, conv_dw, conv_dw_b, conv_ln_g, conv_ln_b, w_conv_out,
           q_norm_g, k_norm_g, w_attn_out, w_out, norm2_g, w_router, b_router,
           w_up_gate, b_up_gate, w_down, b_down):
    b, s, d = h.shape
    t = b * s
    assert s % MOBA_BLOCK == 0 and s // MOBA_BLOCK <= GATE_SLOTS
    tm = _tile(s, 512)
    x2 = h.reshape(t, d)
    hglu, q, k, v, gates, gt = _inproj(x2, norm1_g, w_in, b_gates, q_norm_g, k_norm_g, s, tm)
    c = hglu.shape[1]
    aw = q.shape[1]
    conv = _conv(hglu.reshape(b, s, c), conv_dw, conv_dw_b, conv_ln_g, conv_ln_b, tm)
    attn = _attention(q.reshape(b, s, aw), k.reshape(b, s, N_HEADS * K_AUG), v.reshape(b, s, aw), gt, tm)
    h1, u2p, topi, topg = _mixout(x2, conv.reshape(t, c), attn.reshape(t, aw), gates, w_conv_out, w_attn_out,
                                  w_out, norm2_g, w_router, b_router, tm)
    n_blocks = -(-(t * TOP_K) // EXPERT_ROWS) + N_EXPERTS
    rank, cnt = _rank(topi, _tile(t, 1024))
    dest, blke, meta = _dest(topi, rank, cnt, tm, n_blocks)
    dest = dest.reshape(t // tm, TOP_K * tm)
    meta = meta.reshape(128)
    xs = _dispatch(dest, u2p, meta, n_blocks * EXPERT_ROWS)
    ys = _experts(blke.reshape(n_blocks), meta, xs, w_up_gate, b_up_gate, w_down, b_down)
    out = _combine(dest, ys, h1, topg)
    return out.reshape(b, s, d)


def kernel(x, norm1_g, w_in, b_gates, conv_dw, conv_dw_b, conv_ln_g, conv_ln_b, w_conv_out, q_norm_g, k_norm_g,
           w_attn_out, w_out, norm2_g, w_router, b_router, w_up_gate, b_up_gate, w_down, b_down):
    h = x
    for l in range(norm1_g.shape[0]):
        h = _layer(h, norm1_g[l], w_in[l], b_gates[l], conv_dw[l], conv_dw_b[l], conv_ln_g[l], conv_ln_b[l],
                   w_conv_out[l], q_norm_g[l], k_norm_g[l], w_attn_out[l], w_out[l], norm2_g[l], w_router[l],
                   b_router[l], w_up_gate[l], b_up_gate[l], w_down[l], b_down[l])
    return h
```

```python
import functools

import jax
import jax.numpy as jnp
from jax import lax
from jax.experimental import pallas as pl
from jax.experimental.pallas import tpu as pltpu

F32 = jnp.float32
BF16 = jnp.bfloat16
I32 = jnp.int32
U32 = jnp.uint32

EPS = 1e-6
CONV_WIDTH = 31
HALO = 32
N_HEADS = 8
HEAD_DIM = 64
MOBA_BLOCK = 256
MOBA_TOPK = 3
GATE_SLOTS = 16
HEADS_PER_GROUP = 4
K_AUG = 128
AUG_POS = 16
V_ROWS = 80
MASKED = -1e30
LOG2E = 1.4426950408889634
M_INIT = -3e38
ATTN_UNROLL = 8
ATTN_LOOKAHEAD = 5
ATTN_SLOTS = 7
N_EXPERTS = 32
TOP_K = 4
SWIGLU_LIMIT = 7.0
SWIGLU_ALPHA = 1.702
EXPERT_ROWS = 512
ROW_CHUNKS = 4
DMA_UNROLL = 8
NEG_INF = float("-inf")
VMEM_LIMIT = 56 * 1024 * 1024


def _cparams(sem):
    return pltpu.CompilerParams(dimension_semantics=sem, vmem_limit_bytes=VMEM_LIMIT)


def _dot(a, b):
    return jnp.dot(a, b, preferred_element_type=F32)


def _dot_nt(a, b):
    return lax.dot_general(a, b, (((1,), (1,)), ((), ())), preferred_element_type=F32)


def _split_bf16(x):
    hi = x.astype(BF16)
    lo = (x - hi.astype(F32)).astype(BF16)
    return hi, lo


def _sigmoid(x):
    return 1.0 / (1.0 + jnp.exp(-x))


def _inproj_kernel(tiles_per_seq, x_ref, g1_ref, wglu_ref, wq_ref, wk_ref, wv_ref, wg_ref, bg_ref,
                   qg_ref, kg_ref, bd_ref, place_ref,
                   h_ref, q_ref, k_ref, v_ref, gate_ref, gt_ref, km_ref):
    i = pl.program_id(0)
    tm = x_ref.shape[0]
    c = h_ref.shape[1]
    aw = q_ref.shape[1]

    @pl.when(i == 0)
    def _():
        km_ref[...] = jnp.zeros_like(km_ref)

    x = x_ref[...]
    ms = jnp.mean(x * x, axis=-1, keepdims=True)
    u = (x * lax.rsqrt(ms + EPS) * g1_ref[...]).astype(BF16)

    glu = _dot(u, wglu_ref[...])
    h_ref[...] = (glu[:, :c] * _sigmoid(glu[:, c:])).astype(BF16)

    bd = bd_ref[...]

    def head_norm(t, g):
        ss = _dot((t * t).astype(BF16), bd)
        return t * lax.rsqrt(ss * (1.0 / HEAD_DIM) + EPS) * g

    qn = head_norm(_dot(u, wq_ref[...]), qg_ref[...]) * (HEAD_DIM ** -0.5 * LOG2E)
    kn = head_norm(_dot(u, wk_ref[...]), kg_ref[...])
    q_hi, q_lo = _split_bf16(qn)
    q_ref[...] = q_hi
    pos = lax.broadcasted_iota(I32, (tm, K_AUG), 0) + (i % tiles_per_seq) * tm
    col = lax.broadcasted_iota(I32, (tm, K_AUG), 1)
    kpos = pos & (MOBA_BLOCK - 1)
    bstart = pos - kpos
    aug = jnp.where((col >= HEAD_DIM) & (col < HEAD_DIM + 3), kpos,
                    jnp.where((col >= HEAD_DIM + 3) & (col < HEAD_DIM + 6), bstart,
                              jnp.where((col - (HEAD_DIM + AUG_POS)) * MOBA_BLOCK == bstart, 1, 0))).astype(F32)
    kp = _dot(kn.astype(BF16), place_ref[...])
    for h in range(N_HEADS):
        k_ref[:, h * K_AUG:(h + 1) * K_AUG] = (kp[:, h * K_AUG:(h + 1) * K_AUG] + aug).astype(BF16)
    v_ref[...] = _dot(u, wv_ref[...]).astype(BF16)
    gate_ref[...] = _sigmoid(_dot(u, wg_ref[...]) + bg_ref[...]).astype(BF16)

    lane_head = lax.broadcasted_iota(I32, (1, aw), 1) // HEAD_DIM
    blocks_per_tile = tm // MOBA_BLOCK
    n0 = (i % tiles_per_seq) * blocks_per_tile
    for j in range(blocks_per_tile):
        kmj = jnp.mean(kn[j * MOBA_BLOCK:(j + 1) * MOBA_BLOCK], axis=0, keepdims=True)
        for h in range(N_HEADS):
            km_ref[pl.ds(h * GATE_SLOTS + n0 + j, 1), :] = jnp.where(lane_head == h, kmj, 0.0)

    km_hi, km_lo = _split_bf16(km_ref[...])
    gt_ref[0] = _dot_nt(km_hi, q_hi) + _dot_nt(km_lo, q_hi) + _dot_nt(km_hi, q_lo)


def _inproj(x2, norm1_g, w_in, b_gates, q_norm_g, k_norm_g, seq, tm):
    t, d = x2.shape
    aw = N_HEADS * HEAD_DIM
    c = (w_in.shape[1] - 3 * aw - 2 * d) // 2
    c1, c2, c3, c4 = 2 * c, 2 * c + aw, 2 * c + 2 * aw, 2 * c + 3 * aw
    wb = w_in.astype(BF16)
    hd = jnp.arange(aw, dtype=I32) // HEAD_DIM
    bd = (hd[:, None] == hd[None, :]).astype(BF16)
    lane = jnp.arange(aw, dtype=I32)
    place = ((lane // HEAD_DIM * K_AUG + lane % HEAD_DIM)[:, None]
             == jnp.arange(N_HEADS * K_AUG, dtype=I32)[None, :]).astype(BF16)
    n_tiles = t // tm
    full = lambda a: pl.BlockSpec(a.shape, lambda i: (0,) * a.ndim)
    row = lambda n: pl.BlockSpec((tm, n), lambda i: (i, 0))
    args = (x2, norm1_g.reshape(1, d), wb[:, :c1], wb[:, c1:c2], wb[:, c2:c3], wb[:, c3:c4], wb[:, c4:],
            b_gates.reshape(1, 2 * d), jnp.tile(q_norm_g, N_HEADS).reshape(1, aw),
            jnp.tile(k_norm_g, N_HEADS).reshape(1, aw), bd, place)
    return pl.pallas_call(
        functools.partial(_inproj_kernel, seq // tm),
        grid=(n_tiles,),
        in_specs=[row(d)] + [full(a) for a in args[1:]],
        out_specs=[row(c), row(aw), row(N_HEADS * K_AUG), row(aw), row(2 * d),
                   pl.BlockSpec((1, N_HEADS * GATE_SLOTS, tm), lambda i: (i, 0, 0))],
        out_shape=[jax.ShapeDtypeStruct((t, c), BF16), jax.ShapeDtypeStruct((t, aw), BF16),
                   jax.ShapeDtypeStruct((t, N_HEADS * K_AUG), BF16), jax.ShapeDtypeStruct((t, aw), BF16),
                   jax.ShapeDtypeStruct((t, 2 * d), BF16),
                   jax.ShapeDtypeStruct((n_tiles, N_HEADS * GATE_SLOTS, tm), F32)],
        scratch_shapes=[pltpu.VMEM((N_HEADS * GATE_SLOTS, aw), F32)],
        compiler_params=_cparams(("arbitrary",)),
        name="inproj",
    )(*args)


def _conv_kernel(h_ref, dw_ref, dwb_ref, lng_ref, lnb_ref, o_ref, win_ref, sh_ref):
    s = pl.program_id(1)
    ts = h_ref.shape[1]

    @pl.when(s == 0)
    def _():
        win_ref[0:HALO, :] = jnp.zeros((HALO, win_ref.shape[1]), F32)

    @pl.when(s != 0)
    def _():
        win_ref[0:HALO, :] = win_ref[ts:ts + HALO, :]

    win_ref[HALO:HALO + ts, :] = h_ref[0].astype(F32)
    span = ts + HALO - 8
    for r in range(1, 8):
        sh_ref[r - 1] = win_ref[r:r + span, :]
    acc = jnp.zeros((ts, win_ref.shape[1]), F32) + dwb_ref[...]
    for w in range(CONV_WIDTH):
        off = HALO - (CONV_WIDTH - 1) + w
        r = off % 8
        tap = win_ref[off:off + ts, :] if r == 0 else sh_ref[r - 1, off - r:off - r + ts, :]
        acc = acc + dw_ref[w:w + 1, :] * tap
    mu = jnp.mean(acc, axis=-1, keepdims=True)
    cen = acc - mu
    var = jnp.mean(cen * cen, axis=-1, keepdims=True)
    y = cen * lax.rsqrt(var + EPS) * lng_ref[...] + lnb_ref[...]
    o_ref[0] = (y * _sigmoid(y)).astype(BF16)


def _conv(h3, dw, dwb, lng, lnb, ts):
    b, s, c = h3.shape
    vec = lambda a: pl.BlockSpec(a.shape, lambda i, j: (0, 0))
    args = (h3, dw, dwb.reshape(1, c), lng.reshape(1, c), lnb.reshape(1, c))
    return pl.pallas_call(
        _conv_kernel,
        grid=(b, s // ts),
        in_specs=[pl.BlockSpec((1, ts, c), lambda i, j: (i, j, 0))] + [vec(a) for a in args[1:]],
        out_specs=pl.BlockSpec((1, ts, c), lambda i, j: (i, j, 0)),
        out_shape=jax.ShapeDtypeStruct((b, s, c), BF16),
        scratch_shapes=[pltpu.VMEM((ts + HALO, c), F32), pltpu.VMEM((7, ts + HALO - 8, c), F32)],
        compiler_params=_cparams(("arbitrary", "arbitrary")),
        name="conv",
    )(*args)


def _attn_kernel(q_ref, k_ref, v_ref, gt_ref, srow_ref, o_ref,
                 vt_ref, caus_ref, w_ref, s_ref, m_ref, acc_ref):
    qb = pl.program_id(2)
    blk = MOBA_BLOCK
    hp = HEADS_PER_GROUP
    n_blocks = v_ref.shape[1] // blk

    @pl.when(qb == 0)
    def _():
        ones_rows = (lax.broadcasted_iota(I32, (V_ROWS - HEAD_DIM, blk), 0) == 0).astype(BF16)
        for n in range(n_blocks):
            vt = v_ref[0, n * blk:(n + 1) * blk, :].astype(F32).T
            for h in range(hp):
                vt_ref[n, h, 0:HEAD_DIM, :] = vt[h * HEAD_DIM:(h + 1) * HEAD_DIM].astype(BF16)
                vt_ref[n, h, HEAD_DIM:V_ROWS, :] = ones_rows
        kpos = lax.broadcasted_iota(I32, (blk, blk), 0)
        qpos = lax.broadcasted_iota(I32, (blk, blk), 1)
        caus_ref[0] = jnp.zeros((blk, blk), F32)
        caus_ref[1] = jnp.where(kpos <= qpos, 0.0, NEG_INF)
        for h in range(hp):
            w_ref[h, HEAD_DIM:HEAD_DIM + AUG_POS, :] = srow_ref[h]
            w_ref[h, HEAD_DIM + AUG_POS + GATE_SLOTS:K_AUG, :] = jnp.zeros(
                (K_AUG - HEAD_DIM - AUG_POS - GATE_SLOTS, blk), BF16)

    qt = q_ref[0].astype(F32).T
    slot = lax.broadcasted_iota(I32, (GATE_SLOTS, blk), 0)
    for h in range(hp):
        g = gt_ref[0, h * GATE_SLOTS:(h + 1) * GATE_SLOTS, :]
        g = jnp.where(slot < qb, g, NEG_INF)
        sel = slot == qb
        for _ in range(MOBA_TOPK):
            mx = jnp.max(g, axis=0, keepdims=True)
            first = jnp.min(jnp.where(g == mx, slot, GATE_SLOTS), axis=0, keepdims=True)
            pick = (slot == first) & (mx > NEG_INF)
            sel = sel | pick
            g = jnp.where(pick, NEG_INF, g)
        w_ref[h, 0:HEAD_DIM, :] = qt[h * HEAD_DIM:(h + 1) * HEAD_DIM].astype(BF16)
        w_ref[h, HEAD_DIM + AUG_POS:HEAD_DIM + AUG_POS + GATE_SLOTS, :] = jnp.where(sel, 0.0, MASKED).astype(BF16)

    m_ref[...] = jnp.full(m_ref.shape, M_INIT, F32)
    acc_ref[...] = jnp.zeros(acc_ref.shape, F32)

    def scores(n, h, slot):
        start = pl.multiple_of(n * blk, blk)
        t = _dot(k_ref[0, pl.ds(start, blk), h * K_AUG:(h + 1) * K_AUG], w_ref[h])
        s_ref[slot] = t + caus_ref[jnp.where(n == qb, 1, 0)]

    def accumulate(n, h, slot):
        t = s_ref[slot]
        m_old = m_ref[h]
        m_new = jnp.maximum(m_old, jnp.max(t, axis=0, keepdims=True))
        p = jnp.exp2(t - m_new).astype(BF16)
        pv = _dot(vt_ref[n, h], p)
        m_ref[h] = m_new
        acc_ref[h] = jnp.exp2(m_old - m_new) * acc_ref[h] + pv

    def run_blocks(first, count):
        units = [(first + u, h) for u in range(count) for h in range(hp)]
        ahead = min(ATTN_LOOKAHEAD, len(units))
        for j in range(ahead):
            scores(*units[j], j % ATTN_SLOTS)
        for j, (n, h) in enumerate(units):
            if j + ahead < len(units):
                scores(*units[j + ahead], (j + ahead) % ATTN_SLOTS)
            accumulate(n, h, j % ATTN_SLOTS)

    def chunk(c, carry):
        run_blocks(c * ATTN_UNROLL, ATTN_UNROLL)
        return carry

    full = (qb + 1) // ATTN_UNROLL
    lax.fori_loop(0, full, chunk, 0)
    for left in range(1, ATTN_UNROLL):
        @pl.when(qb + 1 - full * ATTN_UNROLL == left)
        def _():
            run_blocks(full * ATTN_UNROLL, left)
    outs = [acc_ref[h, 0:HEAD_DIM, :] / acc_ref[h, HEAD_DIM:HEAD_DIM + 1, :] for h in range(hp)]
    o_ref[0] = jnp.concatenate(outs, axis=0).T.astype(BF16)


def _slope_rows():
    slopes = 2.0 ** (-8.0 * jnp.arange(1, N_HEADS + 1, dtype=F32) / N_HEADS) * LOG2E
    s1 = slopes.astype(BF16)
    r1 = slopes - s1.astype(F32)
    s2 = r1.astype(BF16)
    s3 = (r1 - s2.astype(F32)).astype(BF16)
    parts = jnp.stack([s1, s2, s3, s1, s2, s3] + [jnp.zeros_like(s1)] * (AUG_POS - 6), axis=1)
    return jnp.broadcast_to(parts[:, :, None], (N_HEADS, AUG_POS, MOBA_BLOCK))


def _attention(q3, kaug3, v3, gt, tm):
    b, s, aw = q3.shape
    blk = MOBA_BLOCK
    hp = HEADS_PER_GROUP
    gw = hp * HEAD_DIM
    n_groups = aw // gw
    nb = s // blk
    assert nb <= GATE_SLOTS
    per_tile = tm // blk
    tiles_per_seq = s // tm
    return pl.pallas_call(
        _attn_kernel,
        grid=(b, n_groups, nb),
        in_specs=[
            pl.BlockSpec((1, blk, gw), lambda i, g, j: (i, j, g)),
            pl.BlockSpec((1, s, hp * K_AUG), lambda i, g, j: (i, 0, g)),
            pl.BlockSpec((1, s, gw), lambda i, g, j: (i, 0, g)),
            pl.BlockSpec((1, hp * GATE_SLOTS, blk),
                         lambda i, g, j: (i * tiles_per_seq + j // per_tile, g, j % per_tile)),
            pl.BlockSpec((hp, AUG_POS, blk), lambda i, g, j: (g, 0, 0)),
        ],
        out_specs=pl.BlockSpec((1, blk, gw), lambda i, g, j: (i, j, g)),
        scratch_shapes=[
            pltpu.VMEM((nb, hp, V_ROWS, blk), BF16),
            pltpu.VMEM((2, blk, blk), F32),
            pltpu.VMEM((hp, K_AUG, blk), BF16),
            pltpu.VMEM((ATTN_SLOTS, blk, blk), F32),
            pltpu.VMEM((hp, 1, blk), F32),
            pltpu.VMEM((hp, V_ROWS, blk), F32),
        ],
        out_shape=jax.ShapeDtypeStruct((b, s, aw), BF16),
        compiler_params=_cparams(("arbitrary", "arbitrary", "arbitrary")),
        name="attn",
    )(q3, kaug3, v3, gt, _slope_rows())


def _pack_bf16_pairs(x):
    n = x.shape[1] // 2
    r = x.astype(BF16).astype(F32)
    lo = lax.shift_right_logical(pltpu.bitcast(r[:, :n], U32), jnp.uint32(16))
    hi = pltpu.bitcast(r[:, n:], U32) & jnp.uint32(0xFFFF0000)
    return hi | lo


def _store_rows(ref, x):
    pk = _pack_bf16_pairs(x)
    m, n = pk.shape
    c = n // 128
    for j in range(c):
        ref[pl.ds(j, m, stride=c), :] = pk[:, j * 128:(j + 1) * 128]


def _load_rows(ref, m):
    c = ref.shape[0] // m
    ws = [ref[pl.ds(j, m, stride=c), :] for j in range(c)]
    lo = [pltpu.bitcast(lax.shift_left(w, jnp.uint32(16)), F32) for w in ws]
    hi = [pltpu.bitcast(w & jnp.uint32(0xFFFF0000), F32) for w in ws]
    return jnp.concatenate(lo + hi, axis=1)


def _mixout_kernel(x_ref, c_ref, a_ref, gate_ref, wco_ref, wao_ref, wout_ref, g2_ref,
                   wr_ref, br_ref,
                   h1_ref, u2_ref, topi_ref, topg_ref):
    d = x_ref.shape[1]
    y_c = _dot(c_ref[...], wco_ref[...])
    y_a = _dot(a_ref[...], wao_ref[...])
    gate = gate_ref[...].astype(F32)
    mixed = (gate[:, :d] * y_c + gate[:, d:] * y_a).astype(BF16)
    h1 = x_ref[...] + _dot(mixed, wout_ref[...])
    h1_ref[...] = h1
    ms = jnp.mean(h1 * h1, axis=-1, keepdims=True)
    u2 = h1 * lax.rsqrt(ms + EPS) * g2_ref[...]
    _store_rows(u2_ref, u2)
    u_hi, u_lo = _split_bf16(u2)
    wr = wr_ref[...]
    both = _dot_nt(wr, u_hi)
    logits = both[:N_EXPERTS] + both[N_EXPERTS:] + _dot_nt(wr[:N_EXPERTS], u_lo) + br_ref[...]
    eid = lax.broadcasted_iota(I32, logits.shape, 0)
    vals = []
    for k in range(TOP_K):
        mx = jnp.max(logits, axis=0, keepdims=True)
        first = jnp.min(jnp.where(logits == mx, eid, N_EXPERTS), axis=0, keepdims=True)
        topi_ref[k:k + 1, :] = first
        vals.append(mx)
        logits = jnp.where(eid == first, NEG_INF, logits)
    es = [jnp.exp(v - vals[0]) for v in vals]
    den = es[0] + es[1] + es[2] + es[3]
    for k in range(TOP_K):
        topg_ref[k:k + 1, :] = es[k] / den


def _mixout(x2, c2, a2, gates, w_conv_out, w_attn_out, w_out, norm2_g, w_router, b_router, tm):
    t, d = x2.shape
    wr_t = w_router.T
    wrh = wr_t.astype(BF16)
    wrl = (wr_t - wrh.astype(F32)).astype(BF16)
    args = (x2, c2, a2, gates, w_conv_out.astype(BF16), w_attn_out.astype(BF16), w_out.astype(BF16),
            norm2_g.reshape(1, d), jnp.concatenate([wrh, wrl], axis=0), b_router.reshape(N_EXPERTS, 1))
    full = lambda a: pl.BlockSpec(a.shape, lambda i: (0,) * a.ndim)
    row = lambda a: pl.BlockSpec((tm, a.shape[1]), lambda i: (i, 0))
    return pl.pallas_call(
        _mixout_kernel,
        grid=(t // tm,),
        in_specs=[row(a) for a in args[:4]] + [full(a) for a in args[4:]],
        out_specs=[pl.BlockSpec((tm, d), lambda i: (i, 0)), pl.BlockSpec((tm * ROW_CHUNKS, 128), lambda i: (i, 0)),
                   pl.BlockSpec((TOP_K, tm), lambda i: (0, i)), pl.BlockSpec((TOP_K, tm), lambda i: (0, i))],
        out_shape=[jax.ShapeDtypeStruct((t, d), F32), jax.ShapeDtypeStruct((t * ROW_CHUNKS, 128), U32),
                   jax.ShapeDtypeStruct((TOP_K, t), I32), jax.ShapeDtypeStruct((TOP_K, t), F32)],
        compiler_params=_cparams(("arbitrary",)),
        name="mixout",
    )(*args)


def _rank_kernel(topi_ref, tri_ref, rank_ref, cnt_ref, run_ref):
    i = pl.program_id(0)
    tr = topi_ref.shape[1]

    @pl.when(i == 0)
    def _():
        run_ref[...] = jnp.zeros_like(run_ref)

    eid = lax.broadcasted_iota(I32, (N_EXPERTS, tr), 0)
    onehots = [(eid == topi_ref[k:k + 1, :]).astype(F32) for k in range(TOP_K)]
    tot = onehots[0] + onehots[1] + onehots[2] + onehots[3]
    before = _dot(tot.astype(BF16), tri_ref[...])
    base = run_ref[...] + before
    for k in range(TOP_K):
        rank_ref[k:k + 1, :] = jnp.sum(onehots[k] * base, axis=0, keepdims=True).astype(I32)
    run_ref[...] = run_ref[...] + jnp.sum(tot, axis=1, keepdims=True)
    cnt_ref[...] = run_ref[...]


def _rank(topi, tr):
    t = topi.shape[1]
    tri = (jnp.arange(tr)[:, None] < jnp.arange(tr)[None, :]).astype(BF16)
    return pl.pallas_call(
        _rank_kernel,
        grid=(t // tr,),
        in_specs=[pl.BlockSpec((TOP_K, tr), lambda i: (0, i)), pl.BlockSpec((tr, tr), lambda i: (0, 0))],
        out_specs=[pl.BlockSpec((TOP_K, tr), lambda i: (0, i)), pl.BlockSpec((N_EXPERTS, 1), lambda i: (0, 0))],
        out_shape=[jax.ShapeDtypeStruct((TOP_K, t), I32), jax.ShapeDtypeStruct((N_EXPERTS, 1), F32)],
        scratch_shapes=[pltpu.VMEM((N_EXPERTS, 1), F32)],
        compiler_params=_cparams(("arbitrary",)),
        name="rank",
    )(topi, tri)


def _dest_kernel(n_blocks, topi_ref, rank_ref, cnt_ref, dest_ref, blke_ref, meta_ref):
    n_sub, _, td = dest_ref.shape
    cnt = cnt_ref[...]
    padded = jnp.floor((cnt + (EXPERT_ROWS - 1)) * (1.0 / EXPERT_ROWS)) * EXPERT_ROWS
    er = lax.broadcasted_iota(I32, (N_EXPERTS, N_EXPERTS), 0)
    ec = lax.broadcasted_iota(I32, (N_EXPERTS, N_EXPERTS), 1)
    pad_row = jnp.sum(jnp.where(er == ec, padded, 0.0), axis=0, keepdims=True)
    start = jnp.sum(jnp.where(ec < er, pad_row, 0.0), axis=1, keepdims=True)
    end = start + padded
    eid = lax.broadcasted_iota(I32, (N_EXPERTS, td), 0)
    for j in range(n_sub):
        for k in range(TOP_K):
            oh = eid == topi_ref[k:k + 1, j * td:(j + 1) * td]
            s_k = jnp.sum(jnp.where(oh, start, 0.0), axis=0, keepdims=True)
            dest_ref[j, k:k + 1, :] = (s_k.astype(I32) + rank_ref[k:k + 1, j * td:(j + 1) * td]) * ROW_CHUNKS
    bstart = (lax.broadcasted_iota(I32, (N_EXPERTS, n_blocks), 1) * EXPERT_ROWS).astype(F32)
    be = jnp.sum((end <= bstart).astype(I32), axis=0, keepdims=True)
    blke_ref[...] = jnp.minimum(be, N_EXPERTS - 1)
    er2 = lax.broadcasted_iota(I32, (N_EXPERTS, 128), 0)
    lane = lax.broadcasted_iota(I32, (N_EXPERTS, 128), 1)
    last = jnp.maximum(end * (1.0 / EXPERT_ROWS) - 1.0, 0.0)
    used = jnp.where(er2 == N_EXPERTS - 1, end * (1.0 / EXPERT_ROWS), 0.0)
    meta = jnp.where(lane == er2, last, 0.0) + jnp.where(lane == N_EXPERTS, used, 0.0)
    meta_ref[...] = jnp.sum(meta, axis=0, keepdims=True).astype(I32)


def _dest(topi, rank, cnt, td, n_blocks):
    t = topi.shape[1]
    n_sub = 4 if t % (4 * td) == 0 else 1
    wide = n_sub * td
    return pl.pallas_call(
        functools.partial(_dest_kernel, n_blocks),
        grid=(t // wide,),
        in_specs=[pl.BlockSpec((TOP_K, wide), lambda i: (0, i)), pl.BlockSpec((TOP_K, wide), lambda i: (0, i)),
                  pl.BlockSpec((N_EXPERTS, 1), lambda i: (0, 0))],
        out_specs=[pl.BlockSpec((n_sub, TOP_K, td), lambda i: (i, 0, 0)),
                   pl.BlockSpec((1, n_blocks), lambda i: (0, 0)), pl.BlockSpec((1, 128), lambda i: (0, 0))],
        out_shape=[jax.ShapeDtypeStruct((t // td, TOP_K, td), I32), jax.ShapeDtypeStruct((1, n_blocks), I32),
                   jax.ShapeDtypeStruct((1, 128), I32)],
        compiler_params=_cparams(("arbitrary",)),
        name="dest",
    )(topi, rank, cnt)


def _row_dma_start(n_tok, row_copy):
    def issue(it, carry):
        for u in range(DMA_UNROLL):
            for k in range(TOP_K):
                row_copy(it * DMA_UNROLL + u, k).start(priority=k % 2)
        return carry

    lax.fori_loop(0, n_tok // DMA_UNROLL, issue, 0)


def _row_dma_wait(n_tok, row_copy):
    def drain(it, carry):
        for u in range(DMA_UNROLL):
            for k in range(TOP_K):
                row_copy(it * DMA_UNROLL + u, k).wait()
        return carry

    lax.fori_loop(0, n_tok // DMA_UNROLL, drain, 0)


def _dest_tables(dest_hbm, tables, sem_idx, n_tiles, use):
    i = pl.program_id(0)
    fetch = lambda tile, slot: pltpu.make_async_copy(dest_hbm.at[tile], tables[slot], sem_idx.at[slot])

    @pl.when(i == 0)
    def _():
        fetch(0, 0).start()

    for slot in range(2):
        @pl.when((i % 2 == slot) & (i < n_tiles))
        def _():
            fetch(i, slot).wait()

            @pl.when(i + 1 < n_tiles)
            def _():
                fetch(i + 1, 1 - slot).start()

            use(slot)


def _dispatch_kernel(dest_hbm, u2_ref, xs_in, xs_hbm, dest_a, dest_b, sem_idx, sem):
    del xs_in
    td = u2_ref.shape[0] // ROW_CHUNKS
    tables = (dest_a, dest_b)

    def scatter(slot):
        def row_copy(t, k):
            src = pl.multiple_of(t * ROW_CHUNKS, ROW_CHUNKS)
            dst = pl.multiple_of(tables[slot][k * td + t], ROW_CHUNKS)
            return pltpu.make_async_copy(u2_ref.at[pl.ds(src, ROW_CHUNKS)], xs_hbm.at[pl.ds(dst, ROW_CHUNKS)], sem)

        _row_dma_start(td, row_copy)
        _row_dma_wait(td, row_copy)

    _dest_tables(dest_hbm, tables, sem_idx, pl.num_programs(0), scatter)


def _zfill_kernel(meta_ref, o_ref):
    del meta_ref
    o_ref[...] = jnp.zeros(o_ref.shape, U32)


def _dispatch(dest, u2p, meta, n_rows):
    n_tiles, per_tile = dest.shape
    td = per_tile // TOP_K
    blk_rows = EXPERT_ROWS * ROW_CHUNKS
    xs0 = pl.pallas_call(
        _zfill_kernel,
        grid_spec=pltpu.PrefetchScalarGridSpec(
            num_scalar_prefetch=1, grid=(N_EXPERTS,), in_specs=[],
            out_specs=pl.BlockSpec((blk_rows, 128), lambda e, mt: (mt[e], 0))),
        out_shape=jax.ShapeDtypeStruct((n_rows * ROW_CHUNKS, 128), U32),
        compiler_params=_cparams(("arbitrary",)),
        name="zfill",
    )(meta)
    return pl.pallas_call(
        _dispatch_kernel,
        grid=(n_tiles,),
        in_specs=[pl.BlockSpec(memory_space=pl.ANY), pl.BlockSpec((td * ROW_CHUNKS, 128), lambda i: (i, 0)),
                  pl.BlockSpec(memory_space=pl.ANY)],
        out_specs=pl.BlockSpec(memory_space=pl.ANY),
        out_shape=jax.ShapeDtypeStruct(xs0.shape, U32),
        scratch_shapes=[pltpu.SMEM((per_tile,), I32), pltpu.SMEM((per_tile,), I32),
                        pltpu.SemaphoreType.DMA((2,)), pltpu.SemaphoreType.DMA],
        input_output_aliases={2: 0},
        compiler_params=_cparams(("arbitrary",)),
        name="dispatch",
    )(dest, u2p, xs0)


def _used_block(i, meta_ref):
    return jnp.minimum(i, meta_ref[N_EXPERTS] - 1)


def _expert_kernel(blke_ref, meta_ref, xs_ref, wug_ref, bug_ref, wd_ref, bd_ref, ys_ref, wug_bf, wd_bf):
    i = pl.program_id(0)
    f = wd_ref.shape[1]
    cur = blke_ref[_used_block(i, meta_ref)]
    prev = blke_ref[_used_block(jnp.maximum(i - 1, 0), meta_ref)]

    @pl.when((i == 0) | (cur != prev))
    def _():
        wug_bf[...] = wug_ref[0].astype(BF16)
        wd_bf[...] = wd_ref[0].astype(BF16)

    @pl.when(i < meta_ref[N_EXPERTS])
    def _():
        x = _load_rows(xs_ref, EXPERT_ROWS).astype(BF16)
        h = _dot(x, wug_bf[...]) + bug_ref[0]
        a = jnp.minimum(h[:, :f], SWIGLU_LIMIT)
        lin = jnp.clip(h[:, f:], -SWIGLU_LIMIT, SWIGLU_LIMIT)
        act = (a * _sigmoid(SWIGLU_ALPHA * a) * (lin + 1.0)).astype(BF16)
        y = _dot(act, wd_bf[...]) + bd_ref[0]
        _store_rows(ys_ref, y)


def _experts(blke, meta, xs, w_up_gate, b_up_gate, w_down, b_down):
    e, d, f2 = w_up_gate.shape
    blk_rows = EXPERT_ROWS * ROW_CHUNKS
    n_blocks = xs.shape[0] // blk_rows
    row_blk = lambda i, be, mt: (_used_block(i, mt), 0)
    expert = lambda i, be, mt: (be[_used_block(i, mt)], 0, 0)
    return pl.pallas_call(
        _expert_kernel,
        grid_spec=pltpu.PrefetchScalarGridSpec(
            num_scalar_prefetch=2,
            grid=(n_blocks,),
            in_specs=[
                pl.BlockSpec((blk_rows, 128), row_blk),
                pl.BlockSpec((1, d, f2), expert),
                pl.BlockSpec((1, 1, f2), expert),
                pl.BlockSpec((1, f2 // 2, d), expert),
                pl.BlockSpec((1, 1, d), expert),
            ],
            out_specs=pl.BlockSpec((blk_rows, 128), row_blk),
            scratch_shapes=[pltpu.VMEM((d, f2), BF16), pltpu.VMEM((f2 // 2, d), BF16)],
        ),
        out_shape=jax.ShapeDtypeStruct(xs.shape, U32),
        compiler_params=_cparams(("arbitrary",)),
        name="experts",
    )(blke, meta, xs, w_up_gate, b_up_gate.reshape(e, 1, f2), w_down, b_down.reshape(e, 1, d))


def _combine_kernel(dest_hbm, ys_hbm, h1_ref, topg_ref, o_ref, *scratch):
    i = pl.program_id(0)
    n_tiles = pl.num_programs(0) - 1
    tc = h1_ref.shape[0]
    dest_smem = scratch[0:2]
    bufs = (scratch[2:2 + TOP_K], scratch[2 + TOP_K:2 + 2 * TOP_K])
    sem_idx, sems = scratch[2 + 2 * TOP_K], scratch[3 + 2 * TOP_K]

    def row_copy(slot, indexed):
        def make(t, k):
            src = pl.multiple_of(dest_smem[slot][k * tc + t], ROW_CHUNKS) if indexed else 0
            dst = pl.multiple_of(t * ROW_CHUNKS, ROW_CHUNKS)
            return pltpu.make_async_copy(ys_hbm.at[pl.ds(src, ROW_CHUNKS)],
                                         bufs[slot][k].at[pl.ds(dst, ROW_CHUNKS)], sems.at[slot])
        return make

    def start(slot):
        _row_dma_start(tc, row_copy(slot, True))

    def finish(slot):
        _row_dma_wait(tc, row_copy(slot, False))
        g = jnp.concatenate([topg_ref[...], jnp.zeros((8 - TOP_K, tc), F32)], axis=0).T
        out = h1_ref[...]
        for k in range(TOP_K):
            out = out + g[:, k:k + 1] * _load_rows(bufs[slot][k], tc)
        o_ref[...] = out

    _dest_tables(dest_hbm, dest_smem, sem_idx, n_tiles, start)

    for slot in range(2):
        @pl.when((i % 2 != slot) & (i > 0))
        def _():
            finish(slot)


def _combine(dest, ys, h1, topg):
    n_tiles, per_tile = dest.shape
    tc = per_tile // TOP_K
    t, d = h1.shape
    prev = lambda i: jnp.maximum(i - 1, 0)
    return pl.pallas_call(
        _combine_kernel,
        grid=(n_tiles + 1,),
        in_specs=[pl.BlockSpec(memory_space=pl.ANY), pl.BlockSpec(memory_space=pl.ANY),
                  pl.BlockSpec((tc, d), lambda i: (prev(i), 0)), pl.BlockSpec((TOP_K, tc), lambda i: (0, prev(i)))],
        out_specs=pl.BlockSpec((tc, d), lambda i: (prev(i), 0)),
        out_shape=jax.ShapeDtypeStruct((t, d), F32),
        scratch_shapes=[pltpu.SMEM((per_tile,), I32)] * 2 + [pltpu.VMEM((tc * ROW_CHUNKS, 128), U32)] * (2 * TOP_K)
                       + [pltpu.SemaphoreType.DMA((2,)), pltpu.SemaphoreType.DMA((2,))],
        compiler_params=_cparams(("arbitrary",)),
        name="combine",
    )(dest, ys, h1, topg)


def _tile(n, pref):
    return pref if n % pref == 0 else n


def _layer(h, norm1_g, w_in, b_gates, conv_dw, conv_dw_b, conv_ln_g, conv_ln_b, w_conv_out,
           q_norm_g, k_norm_g, w_attn_out, w_out, norm2_g, w_router, b_router,
           w_up_gate, b_up_gate, w_down, b_down):
    b, s, d = h.shape
    t = b * s
    assert s % MOBA_BLOCK == 0 and s // MOBA_BLOCK <= GATE_SLOTS
    tm = _tile(s, 512)
    x2 = h.reshape(t, d)
    hglu, q, k, v, gates, gt = _inproj(x2, norm1_g, w_in, b_gates, q_norm_g, k_norm_g, s, tm)
    c = hglu.shape[1]
    aw = q.shape[1]
    conv = _conv(hglu.reshape(b, s, c), conv_dw, conv_dw_b, conv_ln_g, conv_ln_b, tm)
    attn = _attention(q.reshape(b, s, aw), k.reshape(b, s, N_HEADS * K_AUG), v.reshape(b, s, aw), gt, tm)
    h1, u2p, topi, topg = _mixout(x2, conv.reshape(t, c), attn.reshape(t, aw), gates, w_conv_out, w_attn_out,
                                  w_out, norm2_g, w_router, b_router, tm)
    n_blocks = -(-(t * TOP_K) // EXPERT_ROWS) + N_EXPERTS
    rank, cnt = _rank(topi, _tile(t, 1024))
    dest, blke, meta = _dest(topi, rank, cnt, tm, n_blocks)
    dest = dest.reshape(t // tm, TOP_K * tm)
    meta = meta.reshape(128)
    xs = _dispatch(dest, u2p, meta, n_blocks * EXPERT_ROWS)
    ys = _experts(blke.reshape(n_blocks), meta, xs, w_up_gate, b_up_gate, w_down, b_down)
    out = _combine(dest, ys, h1, topg)
    return out.reshape(b, s, d)


def kernel(x, norm1_g, w_in, b_gates, conv_dw, conv_dw_b, conv_ln_g, conv_ln_b, w_conv_out, q_norm_g, k_norm_g,
           w_attn_out, w_out, norm2_g, w_router, b_router, w_up_gate, b_up_gate, w_down, b_down):
    h = x
    for l in range(norm1_g.shape[0]):
        h = _layer(h, norm1_g[l], w_in[l], b_gates[l], conv_dw[l], conv_dw_b[l], conv_ln_g[l], conv_ln_b[l],
                   w_conv_out[l], q_norm_g[l], k_norm_g[l], w_attn_out[l], w_out[l], norm2_g[l], w_router[l],
                   b_router[l], w_up_gate[l], b_up_gate[l], w_down[l], b_down[l])
    return h
```

```python
import functools

import jax
import jax.numpy as jnp
from jax import lax
from jax.experimental import pallas as pl
from jax.experimental.pallas import tpu as pltpu

F32 = jnp.float32
BF16 = jnp.bfloat16
I32 = jnp.int32
U32 = jnp.uint32

EPS = 1e-6
CONV_WIDTH = 31
HALO = 32
N_HEADS = 8
HEAD_DIM = 64
MOBA_BLOCK = 256
MOBA_TOPK = 3
GATE_SLOTS = 16
HEADS_PER_GROUP = 4
K_AUG = 128
AUG_POS = 16
V_ROWS = 80
MASKED = -1e30
LOG2E = 1.4426950408889634
M_INIT = -3e38
ATTN_UNROLL = 8
ATTN_LOOKAHEAD = 5
ATTN_SLOTS = 7
N_EXPERTS = 32
TOP_K = 4
SWIGLU_LIMIT = 7.0
SWIGLU_ALPHA = 1.702
EXPERT_ROWS = 1024
ROW_CHUNKS = 4
DMA_UNROLL = 8
NEG_INF = float("-inf")
VMEM_LIMIT = 56 * 1024 * 1024


def _cparams(sem):
    return pltpu.CompilerParams(dimension_semantics=sem, vmem_limit_bytes=VMEM_LIMIT)


def _dot(a, b):
    return jnp.dot(a, b, preferred_element_type=F32)


def _dot_nt(a, b):
    return lax.dot_general(a, b, (((1,), (1,)), ((), ())), preferred_element_type=F32)


def _split_bf16(x):
    hi = x.astype(BF16)
    lo = (x - hi.astype(F32)).astype(BF16)
    return hi, lo


def _sigmoid(x):
    return 1.0 / (1.0 + jnp.exp(-x))


def _inproj_kernel(tiles_per_seq, x_ref, g1_ref, wglu_ref, wq_ref, wk_ref, wv_ref, wg_ref, bg_ref,
                   qg_ref, kg_ref, bd_ref, place_ref,
                   h_ref, q_ref, k_ref, v_ref, gate_ref, gt_ref, km_ref):
    i = pl.program_id(0)
    tm = x_ref.shape[0]
    c = h_ref.shape[1]
    aw = q_ref.shape[1]

    @pl.when(i == 0)
    def _():
        km_ref[...] = jnp.zeros_like(km_ref)

    x = x_ref[...]
    ms = jnp.mean(x * x, axis=-1, keepdims=True)
    u = (x * lax.rsqrt(ms + EPS) * g1_ref[...]).astype(BF16)

    glu = _dot(u, wglu_ref[...])
    h_ref[...] = (glu[:, :c] * _sigmoid(glu[:, c:])).astype(BF16)

    bd = bd_ref[...]

    def head_norm(t, g):
        ss = _dot((t * t).astype(BF16), bd)
        return t * lax.rsqrt(ss * (1.0 / HEAD_DIM) + EPS) * g

    qn = head_norm(_dot(u, wq_ref[...]), qg_ref[...]) * (HEAD_DIM ** -0.5 * LOG2E)
    kn = head_norm(_dot(u, wk_ref[...]), kg_ref[...])
    q_hi, q_lo = _split_bf16(qn)
    q_ref[...] = q_hi
    pos = lax.broadcasted_iota(I32, (tm, K_AUG), 0) + (i % tiles_per_seq) * tm
    col = lax.broadcasted_iota(I32, (tm, K_AUG), 1)
    kpos = pos & (MOBA_BLOCK - 1)
    bstart = pos - kpos
    aug = jnp.where((col >= HEAD_DIM) & (col < HEAD_DIM + 3), kpos,
                    jnp.where((col >= HEAD_DIM + 3) & (col < HEAD_DIM + 6), bstart,
                              jnp.where((col - (HEAD_DIM + AUG_POS)) * MOBA_BLOCK == bstart, 1, 0))).astype(F32)
    kp = _dot(kn.astype(BF16), place_ref[...])
    for h in range(N_HEADS):
        k_ref[:, h * K_AUG:(h + 1) * K_AUG] = (kp[:, h * K_AUG:(h + 1) * K_AUG] + aug).astype(BF16)
    v_ref[...] = _dot(u, wv_ref[...]).astype(BF16)
    gate_ref[...] = _sigmoid(_dot(u, wg_ref[...]) + bg_ref[...]).astype(BF16)

    lane_head = lax.broadcasted_iota(I32, (1, aw), 1) // HEAD_DIM
    blocks_per_tile = tm // MOBA_BLOCK
    n0 = (i % tiles_per_seq) * blocks_per_tile
    for j in range(blocks_per_tile):
        kmj = jnp.mean(kn[j * MOBA_BLOCK:(j + 1) * MOBA_BLOCK], axis=0, keepdims=True)
        for h in range(N_HEADS):
            km_ref[pl.ds(h * GATE_SLOTS + n0 + j, 1), :] = jnp.where(lane_head == h, kmj, 0.0)

    km_hi, km_lo = _split_bf16(km_ref[...])
    gt_ref[0] = _dot_nt(km_hi, q_hi) + _dot_nt(km_lo, q_hi) + _dot_nt(km_hi, q_lo)


def _inproj(x2, norm1_g, w_in, b_gates, q_norm_g, k_norm_g, seq, tm):
    t, d = x2.shape
    aw = N_HEADS * HEAD_DIM
    c = (w_in.shape[1] - 3 * aw - 2 * d) // 2
    c1, c2, c3, c4 = 2 * c, 2 * c + aw, 2 * c + 2 * aw, 2 * c + 3 * aw
    wb = w_in.astype(BF16)
    hd = jnp.arange(aw, dtype=I32) // HEAD_DIM
    bd = (hd[:, None] == hd[None, :]).astype(BF16)
    lane = jnp.arange(aw, dtype=I32)
    place = ((lane // HEAD_DIM * K_AUG + lane % HEAD_DIM)[:, None]
             == jnp.arange(N_HEADS * K_AUG, dtype=I32)[None, :]).astype(BF16)
    n_tiles = t // tm
    full = lambda a: pl.BlockSpec(a.shape, lambda i: (0,) * a.ndim)
    row = lambda n: pl.BlockSpec((tm, n), lambda i: (i, 0))
    args = (x2, norm1_g.reshape(1, d), wb[:, :c1], wb[:, c1:c2], wb[:, c2:c3], wb[:, c3:c4], wb[:, c4:],
            b_gates.reshape(1, 2 * d), jnp.tile(q_norm_g, N_HEADS).reshape(1, aw),
            jnp.tile(k_norm_g, N_HEADS).reshape(1, aw), bd, place)
    return pl.pallas_call(
        functools.partial(_inproj_kernel, seq // tm),
        grid=(n_tiles,),
        in_specs=[row(d)] + [full(a) for a in args[1:]],
        out_specs=[row(c), row(aw), row(N_HEADS * K_AUG), row(aw), row(2 * d),
                   pl.BlockSpec((1, N_HEADS * GATE_SLOTS, tm), lambda i: (i, 0, 0))],
        out_shape=[jax.ShapeDtypeStruct((t, c), BF16), jax.ShapeDtypeStruct((t, aw), BF16),
                   jax.ShapeDtypeStruct((t, N_HEADS * K_AUG), BF16), jax.ShapeDtypeStruct((t, aw), BF16),
                   jax.ShapeDtypeStruct((t, 2 * d), BF16),
                   jax.ShapeDtypeStruct((n_tiles, N_HEADS * GATE_SLOTS, tm), F32)],
        scratch_shapes=[pltpu.VMEM((N_HEADS * GATE_SLOTS, aw), F32)],
        compiler_params=_cparams(("arbitrary",)),
        name="inproj",
    )(*args)


def _conv_kernel(h_ref, dw_ref, dwb_ref, lng_ref, lnb_ref, o_ref, win_ref, sh_ref):
    s = pl.program_id(1)
    ts = h_ref.shape[1]

    @pl.when(s == 0)
    def _():
        win_ref[0:HALO, :] = jnp.zeros((HALO, win_ref.shape[1]), F32)

    @pl.when(s != 0)
    def _():
        win_ref[0:HALO, :] = win_ref[ts:ts + HALO, :]

    win_ref[HALO:HALO + ts, :] = h_ref[0].astype(F32)
    span = ts + HALO - 8
    for r in range(1, 8):
        sh_ref[r - 1] = win_ref[r:r + span, :]
    acc = jnp.zeros((ts, win_ref.shape[1]), F32) + dwb_ref[...]
    for w in range(CONV_WIDTH):
        off = HALO - (CONV_WIDTH - 1) + w
        r = off % 8
        tap = win_ref[off:off + ts, :] if r == 0 else sh_ref[r - 1, off - r:off - r + ts, :]
        acc = acc + dw_ref[w:w + 1, :] * tap
    mu = jnp.mean(acc, axis=-1, keepdims=True)
    cen = acc - mu
    var = jnp.mean(cen * cen, axis=-1, keepdims=True)
    y = cen * lax.rsqrt(var + EPS) * lng_ref[...] + lnb_ref[...]
    o_ref[0] = (y * _sigmoid(y)).astype(BF16)


def _conv(h3, dw, dwb, lng, lnb, ts):
    b, s, c = h3.shape
    vec = lambda a: pl.BlockSpec(a.shape, lambda i, j: (0, 0))
    args = (h3, dw, dwb.reshape(1, c), lng.reshape(1, c), lnb.reshape(1, c))
    return pl.pallas_call(
        _conv_kernel,
        grid=(b, s // ts),
        in_specs=[pl.BlockSpec((1, ts, c), lambda i, j: (i, j, 0))] + [vec(a) for a in args[1:]],
        out_specs=pl.BlockSpec((1, ts, c), lambda i, j: (i, j, 0)),
        out_shape=jax.ShapeDtypeStruct((b, s, c), BF16),
        scratch_shapes=[pltpu.VMEM((ts + HALO, c), F32), pltpu.VMEM((7, ts + HALO - 8, c), F32)],
        compiler_params=_cparams(("arbitrary", "arbitrary")),
        name="conv",
    )(*args)


def _attn_kernel(q_ref, k_ref, v_ref, gt_ref, srow_ref, o_ref,
                 vt_ref, caus_ref, w_ref, s_ref, m_ref, acc_ref):
    qb = pl.program_id(2)
    blk = MOBA_BLOCK
    hp = HEADS_PER_GROUP
    n_blocks = v_ref.shape[1] // blk

    @pl.when(qb == 0)
    def _():
        ones_rows = (lax.broadcasted_iota(I32, (V_ROWS - HEAD_DIM, blk), 0) == 0).astype(BF16)
        for n in range(n_blocks):
            vt = v_ref[0, n * blk:(n + 1) * blk, :].astype(F32).T
            for h in range(hp):
                vt_ref[n, h, 0:HEAD_DIM, :] = vt[h * HEAD_DIM:(h + 1) * HEAD_DIM].astype(BF16)
                vt_ref[n, h, HEAD_DIM:V_ROWS, :] = ones_rows
        kpos = lax.broadcasted_iota(I32, (blk, blk), 0)
        qpos = lax.broadcasted_iota(I32, (blk, blk), 1)
        caus_ref[0] = jnp.zeros((blk, blk), F32)
        caus_ref[1] = jnp.where(kpos <= qpos, 0.0, NEG_INF)
        for h in range(hp):
            w_ref[h, HEAD_DIM:HEAD_DIM + AUG_POS, :] = srow_ref[h]
            w_ref[h, HEAD_DIM + AUG_POS + GATE_SLOTS:K_AUG, :] = jnp.zeros(
                (K_AUG - HEAD_DIM - AUG_POS - GATE_SLOTS, blk), BF16)

    qt = q_ref[0].astype(F32).T
    slot = lax.broadcasted_iota(I32, (GATE_SLOTS, blk), 0)
    for h in range(hp):
        g = gt_ref[0, h * GATE_SLOTS:(h + 1) * GATE_SLOTS, :]
        g = jnp.where(slot < qb, g, NEG_INF)
        sel = slot == qb
        for _ in range(MOBA_TOPK):
            mx = jnp.max(g, axis=0, keepdims=True)
            first = jnp.min(jnp.where(g == mx, slot, GATE_SLOTS), axis=0, keepdims=True)
            pick = (slot == first) & (mx > NEG_INF)
            sel = sel | pick
            g = jnp.where(pick, NEG_INF, g)
        w_ref[h, 0:HEAD_DIM, :] = qt[h * HEAD_DIM:(h + 1) * HEAD_DIM].astype(BF16)
        w_ref[h, HEAD_DIM + AUG_POS:HEAD_DIM + AUG_POS + GATE_SLOTS, :] = jnp.where(sel, 0.0, MASKED).astype(BF16)

    m_ref[...] = jnp.full(m_ref.shape, M_INIT, F32)
    acc_ref[...] = jnp.zeros(acc_ref.shape, F32)

    def scores(n, h, slot):
        start = pl.multiple_of(n * blk, blk)
        t = _dot(k_ref[0, pl.ds(start, blk), h * K_AUG:(h + 1) * K_AUG], w_ref[h])
        s_ref[slot] = t + caus_ref[jnp.where(n == qb, 1, 0)]

    def accumulate(n, h, slot):
        t = s_ref[slot]
        m_old = m_ref[h]
        m_new = jnp.maximum(m_old, jnp.max(t, axis=0, keepdims=True))
        p = jnp.exp2(t - m_new).astype(BF16)
        pv = _dot(vt_ref[n, h], p)
        m_ref[h] = m_new
        acc_ref[h] = jnp.exp2(m_old - m_new) * acc_ref[h] + pv

    def run_blocks(first, count):
        units = [(first + u, h) for u in range(count) for h in range(hp)]
        ahead = min(ATTN_LOOKAHEAD, len(units))
        for j in range(ahead):
            scores(*units[j], j % ATTN_SLOTS)
        for j, (n, h) in enumerate(units):
            if j + ahead < len(units):
                scores(*units[j + ahead], (j + ahead) % ATTN_SLOTS)
            accumulate(n, h, j % ATTN_SLOTS)

    def chunk(c, carry):
        run_blocks(c * ATTN_UNROLL, ATTN_UNROLL)
        return carry

    full = (qb + 1) // ATTN_UNROLL
    lax.fori_loop(0, full, chunk, 0)
    for left in range(1, ATTN_UNROLL):
        @pl.when(qb + 1 - full * ATTN_UNROLL == left)
        def _():
            run_blocks(full * ATTN_UNROLL, left)
    outs = [acc_ref[h, 0:HEAD_DIM, :] / acc_ref[h, HEAD_DIM:HEAD_DIM + 1, :] for h in range(hp)]
    o_ref[0] = jnp.concatenate(outs, axis=0).T.astype(BF16)


def _slope_rows():
    slopes = 2.0 ** (-8.0 * jnp.arange(1, N_HEADS + 1, dtype=F32) / N_HEADS) * LOG2E
    s1 = slopes.astype(BF16)
    r1 = slopes - s1.astype(F32)
    s2 = r1.astype(BF16)
    s3 = (r1 - s2.astype(F32)).astype(BF16)
    parts = jnp.stack([s1, s2, s3, s1, s2, s3] + [jnp.zeros_like(s1)] * (AUG_POS - 6), axis=1)
    return jnp.broadcast_to(parts[:, :, None], (N_HEADS, AUG_POS, MOBA_BLOCK))


def _attention(q3, kaug3, v3, gt, tm):
    b, s, aw = q3.shape
    blk = MOBA_BLOCK
    hp = HEADS_PER_GROUP
    gw = hp * HEAD_DIM
    n_groups = aw // gw
    nb = s // blk
    assert nb <= GATE_SLOTS
    per_tile = tm // blk
    tiles_per_seq = s // tm
    return pl.pallas_call(
        _attn_kernel,
        grid=(b, n_groups, nb),
        in_specs=[
            pl.BlockSpec((1, blk, gw), lambda i, g, j: (i, j, g)),
            pl.BlockSpec((1, s, hp * K_AUG), lambda i, g, j: (i, 0, g)),
            pl.BlockSpec((1, s, gw), lambda i, g, j: (i, 0, g)),
            pl.BlockSpec((1, hp * GATE_SLOTS, blk),
                         lambda i, g, j: (i * tiles_per_seq + j // per_tile, g, j % per_tile)),
            pl.BlockSpec((hp, AUG_POS, blk), lambda i, g, j: (g, 0, 0)),
        ],
        out_specs=pl.BlockSpec((1, blk, gw), lambda i, g, j: (i, j, g)),
        scratch_shapes=[
            pltpu.VMEM((nb, hp, V_ROWS, blk), BF16),
            pltpu.VMEM((2, blk, blk), F32),
            pltpu.VMEM((hp, K_AUG, blk), BF16),
            pltpu.VMEM((ATTN_SLOTS, blk, blk), F32),
            pltpu.VMEM((hp, 1, blk), F32),
            pltpu.VMEM((hp, V_ROWS, blk), F32),
        ],
        out_shape=jax.ShapeDtypeStruct((b, s, aw), BF16),
        compiler_params=_cparams(("arbitrary", "arbitrary", "arbitrary")),
        name="attn",
    )(q3, kaug3, v3, gt, _slope_rows())


def _pack_bf16_pairs(x):
    n = x.shape[1] // 2
    r = x.astype(BF16).astype(F32)
    lo = lax.shift_right_logical(pltpu.bitcast(r[:, :n], U32), jnp.uint32(16))
    hi = pltpu.bitcast(r[:, n:], U32) & jnp.uint32(0xFFFF0000)
    return hi | lo


def _store_rows(ref, x):
    pk = _pack_bf16_pairs(x)
    m, n = pk.shape
    c = n // 128
    for j in range(c):
        ref[pl.ds(j, m, stride=c), :] = pk[:, j * 128:(j + 1) * 128]


def _load_rows(ref, m):
    c = ref.shape[0] // m
    ws = [ref[pl.ds(j, m, stride=c), :] for j in range(c)]
    lo = [pltpu.bitcast(lax.shift_left(w, jnp.uint32(16)), F32) for w in ws]
    hi = [pltpu.bitcast(w & jnp.uint32(0xFFFF0000), F32) for w in ws]
    return jnp.concatenate(lo + hi, axis=1)


def _mixout_kernel(x_ref, c_ref, a_ref, gate_ref, wco_ref, wao_ref, wout_ref, g2_ref,
                   wr_ref, br_ref,
                   h1_ref, u2_ref, topi_ref, topg_ref):
    d = x_ref.shape[1]
    y_c = _dot(c_ref[...], wco_ref[...])
    y_a = _dot(a_ref[...], wao_ref[...])
    gate = gate_ref[...].astype(F32)
    mixed = (gate[:, :d] * y_c + gate[:, d:] * y_a).astype(BF16)
    h1 = x_ref[...] + _dot(mixed, wout_ref[...])
    h1_ref[...] = h1
    ms = jnp.mean(h1 * h1, axis=-1, keepdims=True)
    u2 = h1 * lax.rsqrt(ms + EPS) * g2_ref[...]
    _store_rows(u2_ref, u2)
    u_hi, u_lo = _split_bf16(u2)
    wr = wr_ref[...]
    both = _dot_nt(wr, u_hi)
    logits = both[:N_EXPERTS] + both[N_EXPERTS:] + _dot_nt(wr[:N_EXPERTS], u_lo) + br_ref[...]
    eid = lax.broadcasted_iota(I32, logits.shape, 0)
    vals = []
    for k in range(TOP_K):
        mx = jnp.max(logits, axis=0, keepdims=True)
        first = jnp.min(jnp.where(logits == mx, eid, N_EXPERTS), axis=0, keepdims=True)
        topi_ref[k:k + 1, :] = first
        vals.append(mx)
        logits = jnp.where(eid == first, NEG_INF, logits)
    es = [jnp.exp(v - vals[0]) for v in vals]
    den = es[0] + es[1] + es[2] + es[3]
    for k in range(TOP_K):
        topg_ref[k:k + 1, :] = es[k] / den


def _mixout(x2, c2, a2, gates, w_conv_out, w_attn_out, w_out, norm2_g, w_router, b_router, tm):
    t, d = x2.shape
    wr_t = w_router.T
    wrh = wr_t.astype(BF16)
    wrl = (wr_t - wrh.astype(F32)).astype(BF16)
    args = (x2, c2, a2, gates, w_conv_out.astype(BF16), w_attn_out.astype(BF16), w_out.astype(BF16),
            norm2_g.reshape(1, d), jnp.concatenate([wrh, wrl], axis=0), b_router.reshape(N_EXPERTS, 1))
    full = lambda a: pl.BlockSpec(a.shape, lambda i: (0,) * a.ndim)
    row = lambda a: pl.BlockSpec((tm, a.shape[1]), lambda i: (i, 0))
    return pl.pallas_call(
        _mixout_kernel,
        grid=(t // tm,),
        in_specs=[row(a) for a in args[:4]] + [full(a) for a in args[4:]],
        out_specs=[pl.BlockSpec((tm, d), lambda i: (i, 0)), pl.BlockSpec((tm * ROW_CHUNKS, 128), lambda i: (i, 0)),
                   pl.BlockSpec((TOP_K, tm), lambda i: (0, i)), pl.BlockSpec((TOP_K, tm), lambda i: (0, i))],
        out_shape=[jax.ShapeDtypeStruct((t, d), F32), jax.ShapeDtypeStruct((t * ROW_CHUNKS, 128), U32),
                   jax.ShapeDtypeStruct((TOP_K, t), I32), jax.ShapeDtypeStruct((TOP_K, t), F32)],
        compiler_params=_cparams(("arbitrary",)),
        name="mixout",
    )(*args)


def _rank_kernel(topi_ref, tri_ref, rank_ref, cnt_ref, run_ref):
    i = pl.program_id(0)
    tr = topi_ref.shape[1]

    @pl.when(i == 0)
    def _():
        run_ref[...] = jnp.zeros_like(run_ref)

    eid = lax.broadcasted_iota(I32, (N_EXPERTS, tr), 0)
    onehots = [(eid == topi_ref[k:k + 1, :]).astype(F32) for k in range(TOP_K)]
    tot = onehots[0] + onehots[1] + onehots[2] + onehots[3]
    before = _dot(tot.astype(BF16), tri_ref[...])
    base = run_ref[...] + before
    for k in range(TOP_K):
        rank_ref[k:k + 1, :] = jnp.sum(onehots[k] * base, axis=0, keepdims=True).astype(I32)
    run_ref[...] = run_ref[...] + jnp.sum(tot, axis=1, keepdims=True)
    cnt_ref[...] = run_ref[...]


def _rank(topi, tr):
    t = topi.shape[1]
    tri = (jnp.arange(tr)[:, None] < jnp.arange(tr)[None, :]).astype(BF16)
    return pl.pallas_call(
        _rank_kernel,
        grid=(t // tr,),
        in_specs=[pl.BlockSpec((TOP_K, tr), lambda i: (0, i)), pl.BlockSpec((tr, tr), lambda i: (0, 0))],
        out_specs=[pl.BlockSpec((TOP_K, tr), lambda i: (0, i)), pl.BlockSpec((N_EXPERTS, 1), lambda i: (0, 0))],
        out_shape=[jax.ShapeDtypeStruct((TOP_K, t), I32), jax.ShapeDtypeStruct((N_EXPERTS, 1), F32)],
        scratch_shapes=[pltpu.VMEM((N_EXPERTS, 1), F32)],
        compiler_params=_cparams(("arbitrary",)),
        name="rank",
    )(topi, tri)


def _dest_kernel(n_blocks, topi_ref, rank_ref, cnt_ref, dest_ref, blke_ref, meta_ref):
    n_sub, _, td = dest_ref.shape
    cnt = cnt_ref[...]
    padded = jnp.floor((cnt + (EXPERT_ROWS - 1)) * (1.0 / EXPERT_ROWS)) * EXPERT_ROWS
    er = lax.broadcasted_iota(I32, (N_EXPERTS, N_EXPERTS), 0)
    ec = lax.broadcasted_iota(I32, (N_EXPERTS, N_EXPERTS), 1)
    pad_row = jnp.sum(jnp.where(er == ec, padded, 0.0), axis=0, keepdims=True)
    start = jnp.sum(jnp.where(ec < er, pad_row, 0.0), axis=1, keepdims=True)
    end = start + padded
    eid = lax.broadcasted_iota(I32, (N_EXPERTS, td), 0)
    for j in range(n_sub):
        for k in range(TOP_K):
            oh = eid == topi_ref[k:k + 1, j * td:(j + 1) * td]
            s_k = jnp.sum(jnp.where(oh, start, 0.0), axis=0, keepdims=True)
            dest_ref[j, k:k + 1, :] = (s_k.astype(I32) + rank_ref[k:k + 1, j * td:(j + 1) * td]) * ROW_CHUNKS
    bstart = (lax.broadcasted_iota(I32, (N_EXPERTS, n_blocks), 1) * EXPERT_ROWS).astype(F32)
    be = jnp.sum((end <= bstart).astype(I32), axis=0, keepdims=True)
    blke_ref[...] = jnp.minimum(be, N_EXPERTS - 1)
    er2 = lax.broadcasted_iota(I32, (N_EXPERTS, 128), 0)
    lane = lax.broadcasted_iota(I32, (N_EXPERTS, 128), 1)
    last = jnp.maximum(end * (1.0 / EXPERT_ROWS) - 1.0, 0.0)
    used = jnp.where(er2 == N_EXPERTS - 1, end * (1.0 / EXPERT_ROWS), 0.0)
    meta = jnp.where(lane == er2, last, 0.0) + jnp.where(lane == N_EXPERTS, used, 0.0)
    meta_ref[...] = jnp.sum(meta, axis=0, keepdims=True).astype(I32)


def _dest(topi, rank, cnt, td, n_blocks):
    t = topi.shape[1]
    n_sub = 4 if t % (4 * td) == 0 else 1
    wide = n_sub * td
    return pl.pallas_call(
        functools.partial(_dest_kernel, n_blocks),
        grid=(t // wide,),
        in_specs=[pl.BlockSpec((TOP_K, wide), lambda i: (0, i)), pl.BlockSpec((TOP_K, wide), lambda i: (0, i)),
                  pl.BlockSpec((N_EXPERTS, 1), lambda i: (0, 0))],
        out_specs=[pl.BlockSpec((n_sub, TOP_K, td), lambda i: (i, 0, 0)),
                   pl.BlockSpec((1, n_blocks), lambda i: (0, 0)), pl.BlockSpec((1, 128), lambda i: (0, 0))],
        out_shape=[jax.ShapeDtypeStruct((t // td, TOP_K, td), I32), jax.ShapeDtypeStruct((1, n_blocks), I32),
                   jax.ShapeDtypeStruct((1, 128), I32)],
        compiler_params=_cparams(("arbitrary",)),
        name="dest",
    )(topi, rank, cnt)


def _row_dma_start(n_tok, row_copy):
    def issue(it, carry):
        for u in range(DMA_UNROLL):
            for k in range(TOP_K):
                row_copy(it * DMA_UNROLL + u, k).start(priority=k % 2)
        return carry

    lax.fori_loop(0, n_tok // DMA_UNROLL, issue, 0)


def _row_dma_wait(n_tok, row_copy):
    def drain(it, carry):
        for u in range(DMA_UNROLL):
            for k in range(TOP_K):
                row_copy(it * DMA_UNROLL + u, k).wait()
        return carry

    lax.fori_loop(0, n_tok // DMA_UNROLL, drain, 0)


def _dest_tables(dest_hbm, tables, sem_idx, n_tiles, use):
    i = pl.program_id(0)
    fetch = lambda tile, slot: pltpu.make_async_copy(dest_hbm.at[tile], tables[slot], sem_idx.at[slot])

    @pl.when(i == 0)
    def _():
        fetch(0, 0).start()

    for slot in range(2):
        @pl.when((i % 2 == slot) & (i < n_tiles))
        def _():
            fetch(i, slot).wait()

            @pl.when(i + 1 < n_tiles)
            def _():
                fetch(i + 1, 1 - slot).start()

            use(slot)


def _dispatch_kernel(dest_hbm, u2_ref, xs_in, xs_hbm, dest_a, dest_b, sem_idx, sem):
    del xs_in
    td = u2_ref.shape[0] // ROW_CHUNKS
    tables = (dest_a, dest_b)

    def scatter(slot):
        def row_copy(t, k):
            src = pl.multiple_of(t * ROW_CHUNKS, ROW_CHUNKS)
            dst = pl.multiple_of(tables[slot][k * td + t], ROW_CHUNKS)
            return pltpu.make_async_copy(u2_ref.at[pl.ds(src, ROW_CHUNKS)], xs_hbm.at[pl.ds(dst, ROW_CHUNKS)], sem)

        _row_dma_start(td, row_copy)
        _row_dma_wait(td, row_copy)

    _dest_tables(dest_hbm, tables, sem_idx, pl.num_programs(0), scatter)


def _zfill_kernel(meta_ref, o_ref):
    del meta_ref
    o_ref[...] = jnp.zeros(o_ref.shape, U32)


def _dispatch(dest, u2p, meta, n_rows):
    n_tiles, per_tile = dest.shape
    td = per_tile // TOP_K
    blk_rows = EXPERT_ROWS * ROW_CHUNKS
    xs0 = pl.pallas_call(
        _zfill_kernel,
        grid_spec=pltpu.PrefetchScalarGridSpec(
            num_scalar_prefetch=1, grid=(N_EXPERTS,), in_specs=[],
            out_specs=pl.BlockSpec((blk_rows, 128), lambda e, mt: (mt[e], 0))),
        out_shape=jax.ShapeDtypeStruct((n_rows * ROW_CHUNKS, 128), U32),
        compiler_params=_cparams(("arbitrary",)),
        name="zfill",
    )(meta)
    return pl.pallas_call(
        _dispatch_kernel,
        grid=(n_tiles,),
        in_specs=[pl.BlockSpec(memory_space=pl.ANY), pl.BlockSpec((td * ROW_CHUNKS, 128), lambda i: (i, 0)),
                  pl.BlockSpec(memory_space=pl.ANY)],
        out_specs=pl.BlockSpec(memory_space=pl.ANY),
        out_shape=jax.ShapeDtypeStruct(xs0.shape, U32),
        scratch_shapes=[pltpu.SMEM((per_tile,), I32), pltpu.SMEM((per_tile,), I32),
                        pltpu.SemaphoreType.DMA((2,)), pltpu.SemaphoreType.DMA],
        input_output_aliases={2: 0},
        compiler_params=_cparams(("arbitrary",)),
        name="dispatch",
    )(dest, u2p, xs0)


def _used_block(i, meta_ref):
    return jnp.minimum(i, meta_ref[N_EXPERTS] - 1)


def _expert_kernel(blke_ref, meta_ref, xs_ref, wug_ref, bug_ref, wd_ref, bd_ref, ys_ref, wug_bf, wd_bf):
    i = pl.program_id(0)
    f = wd_ref.shape[1]
    cur = blke_ref[_used_block(i, meta_ref)]
    prev = blke_ref[_used_block(jnp.maximum(i - 1, 0), meta_ref)]

    @pl.when((i == 0) | (cur != prev))
    def _():
        wug_bf[...] = wug_ref[0].astype(BF16)
        wd_bf[...] = wd_ref[0].astype(BF16)

    @pl.when(i < meta_ref[N_EXPERTS])
    def _():
        x = _load_rows(xs_ref, EXPERT_ROWS).astype(BF16)
        h = _dot(x, wug_bf[...]) + bug_ref[0]
        a = jnp.minimum(h[:, :f], SWIGLU_LIMIT)
        lin = jnp.clip(h[:, f:], -SWIGLU_LIMIT, SWIGLU_LIMIT)
        act = (a * _sigmoid(SWIGLU_ALPHA * a) * (lin + 1.0)).astype(BF16)
        y = _dot(act, wd_bf[...]) + bd_ref[0]
        _store_rows(ys_ref, y)


def _experts(blke, meta, xs, w_up_gate, b_up_gate, w_down, b_down):
    e, d, f2 = w_up_gate.shape
    blk_rows = EXPERT_ROWS * ROW_CHUNKS
    n_blocks = xs.shape[0] // blk_rows
    row_blk = lambda i, be, mt: (_used_block(i, mt), 0)
    expert = lambda i, be, mt: (be[_used_block(i, mt)], 0, 0)
    return pl.pallas_call(
        _expert_kernel,
        grid_spec=pltpu.PrefetchScalarGridSpec(
            num_scalar_prefetch=2,
            grid=(n_blocks,),
            in_specs=[
                pl.BlockSpec((blk_rows, 128), row_blk),
                pl.BlockSpec((1, d, f2), expert),
                pl.BlockSpec((1, 1, f2), expert),
                pl.BlockSpec((1, f2 // 2, d), expert),
                pl.BlockSpec((1, 1, d), expert),
            ],
            out_specs=pl.BlockSpec((blk_rows, 128), row_blk),
            scratch_shapes=[pltpu.VMEM((d, f2), BF16), pltpu.VMEM((f2 // 2, d), BF16)],
        ),
        out_shape=jax.ShapeDtypeStruct(xs.shape, U32),
        compiler_params=_cparams(("arbitrary",)),
        name="experts",
    )(blke, meta, xs, w_up_gate, b_up_gate.reshape(e, 1, f2), w_down, b_down.reshape(e, 1, d))


def _combine_kernel(dest_hbm, ys_hbm, h1_ref, topg_ref, o_ref, *scratch):
    i = pl.program_id(0)
    n_tiles = pl.num_programs(0) - 1
    tc = h1_ref.shape[0]
    dest_smem = scratch[0:2]
    bufs = (scratch[2:2 + TOP_K], scratch[2 + TOP_K:2 + 2 * TOP_K])
    sem_idx, sems = scratch[2 + 2 * TOP_K], scratch[3 + 2 * TOP_K]

    def row_copy(slot, indexed):
        def make(t, k):
            src = pl.multiple_of(dest_smem[slot][k * tc + t], ROW_CHUNKS) if indexed else 0
            dst = pl.multiple_of(t * ROW_CHUNKS, ROW_CHUNKS)
            return pltpu.make_async_copy(ys_hbm.at[pl.ds(src, ROW_CHUNKS)],
                                         bufs[slot][k].at[pl.ds(dst, ROW_CHUNKS)], sems.at[slot])
        return make

    def start(slot):
        _row_dma_start(tc, row_copy(slot, True))

    def finish(slot):
        _row_dma_wait(tc, row_copy(slot, False))
        g = jnp.concatenate([topg_ref[...], jnp.zeros((8 - TOP_K, tc), F32)], axis=0).T
        out = h1_ref[...]
        for k in range(TOP_K):
            out = out + g[:, k:k + 1] * _load_rows(bufs[slot][k], tc)
        o_ref[...] = out

    _dest_tables(dest_hbm, dest_smem, sem_idx, n_tiles, start)

    for slot in range(2):
        @pl.when((i % 2 != slot) & (i > 0))
        def _():
            finish(slot)


def _combine(dest, ys, h1, topg):
    n_tiles, per_tile = dest.shape
    tc = per_tile // TOP_K
    t, d = h1.shape
    prev = lambda i: jnp.maximum(i - 1, 0)
    return pl.pallas_call(
        _combine_kernel,
        grid=(n_tiles + 1,),
        in_specs=[pl.BlockSpec(memory_space=pl.ANY), pl.BlockSpec(memory_space=pl.ANY),
                  pl.BlockSpec((tc, d), lambda i: (prev(i), 0)), pl.BlockSpec((TOP_K, tc), lambda i: (0, prev(i)))],
        out_specs=pl.BlockSpec((tc, d), lambda i: (prev(i), 0)),
        out_shape=jax.ShapeDtypeStruct((t, d), F32),
        scratch_shapes=[pltpu.SMEM((per_tile,), I32)] * 2 + [pltpu.VMEM((tc * ROW_CHUNKS, 128), U32)] * (2 * TOP_K)
                       + [pltpu.SemaphoreType.DMA((2,)), pltpu.SemaphoreType.DMA((2,))],
        compiler_params=_cparams(("arbitrary",)),
        name="combine",
    )(dest, ys, h1, topg)


def _tile(n, pref):
    return pref if n % pref == 0 else n


def _layer(h, norm1_g, w_in, b_gates, conv_dw, conv_dw_b, conv_ln_g, conv_ln_b, w_conv_out,
           q_norm_g, k_norm_g, w_attn_out, w_out, norm2_g, w_router, b_router,
           w_up_gate, b_up_gate, w_down, b_down):
    b, s, d = h.shape
    t = b * s
    assert s % MOBA_BLOCK == 0 and s // MOBA_BLOCK <= GATE_SLOTS
    tm = _tile(s, 512)
    x2 = h.reshape(t, d)
    hglu, q, k, v, gates, gt = _inproj(x2, norm1_g, w_in, b_gates, q_norm_g, k_norm_g, s, tm)
    c = hglu.shape[1]
    aw = q.shape[1]
    conv = _conv(hglu.reshape(b, s, c), conv_dw, conv_dw_b, conv_ln_g, conv_ln_b, tm)
    attn = _attention(q.reshape(b, s, aw), k.reshape(b, s, N_HEADS * K_AUG), v.reshape(b, s, aw), gt, tm)
    h1, u2p, topi, topg = _mixout(x2, conv.reshape(t, c), attn.reshape(t, aw), gates, w_conv_out, w_attn_out,
                                  w_out, norm2_g, w_router, b_router, tm)
    n_blocks = -(-(t * TOP_K) // EXPERT_ROWS) + N_EXPERTS
    rank, cnt = _rank(topi, _tile(t, 1024))
    dest, blke, meta = _dest(topi, rank, cnt, tm, n_blocks)
    dest = dest.reshape(t // tm, TOP_K * tm)
    meta = meta.reshape(128)
    xs = _dispatch(dest, u2p, meta, n_blocks * EXPERT_ROWS)
    ys = _experts(blke.reshape(n_blocks), meta, xs, w_up_gate, b_up_gate, w_down, b_down)
    out = _combine(dest, ys, h1, topg)
    return out.reshape(b, s, d)


def kernel(x, norm1_g, w_in, b_gates, conv_dw, conv_dw_b, conv_ln_g, conv_ln_b, w_conv_out, q_norm_g, k_norm_g,
           w_attn_out, w_out, norm2_g, w_router, b_router, w_up_gate, b_up_gate, w_down, b_down):
    h = x
    for l in range(norm1_g.shape[0]):
        h = _layer(h, norm1_g[l], w_in[l], b_gates[l], conv_dw[l], conv_dw_b[l], conv_ln_g[l], conv_ln_b[l],
                   w_conv_out[l], q_norm_g[l], k_norm_g[l], w_attn_out[l], w_out[l], norm2_g[l], w_router[l],
                   b_router[l], w_up_gate[l], b_up_gate[l], w_down[l], b_down[l])
    return h
```
